```python
import math
import jax
import jax.numpy as jnp
from jax import lax
import numpy as np

D_MODEL = 2048
BATCH = 4
SEQ = 4096
DEPTH = 4

CTX_LEN = 256
GRID_W = 64
EPS = 1e-6
N_MOD = 6

ATT_HEADS = 8
ATT_QK_DIM = 64
ATT_V_DIM = 2 * ATT_QK_DIM
ATT_WIDTH = ATT_HEADS * ATT_V_DIM
Q_BLOCK = 128
ROPE_THETA = 10000.0
ROPE_PAIRS_PER_AXIS = ATT_QK_DIM // 4

SSM_WIDTH = D_MODEL - ATT_WIDTH
SSM_HEAD_DIM = 64
SSM_HEADS = SSM_WIDTH // SSM_HEAD_DIM
SSM_GROUPS = 2
SSM_HEADS_PER_GROUP = SSM_HEADS // SSM_GROUPS
SSM_STATE = 128
SSM_CONV = 5
SSM_CHUNK = 128

MIX_WIDTH = ATT_WIDTH + SSM_WIDTH

Q_COLS = ATT_HEADS * 2 * ATT_QK_DIM
K_COLS = ATT_HEADS * 2 * ATT_QK_DIM
V_COLS = ATT_WIDTH
Z_COLS = SSM_WIDTH
XBC_COLS = SSM_WIDTH + 2 * SSM_GROUPS * SSM_STATE
DT_COLS = 2 * SSM_HEADS
IN_COLS = Q_COLS + K_COLS + V_COLS + Z_COLS + XBC_COLS + DT_COLS
IN_SPLITS = (Q_COLS,
             Q_COLS + K_COLS,
             Q_COLS + K_COLS + V_COLS,
             Q_COLS + K_COLS + V_COLS + Z_COLS,
             Q_COLS + K_COLS + V_COLS + Z_COLS + XBC_COLS)

D_FF = 5632
N_EXPERTS = 8
TOP_K = 2
D_FF_EXPERT = 5632

kernel_name = "hybrid_diffattn_ssd_moe_dit"


def rmsnorm(x, g):
    xf = x.astype(jnp.float32)
    xf = xf * lax.rsqrt(jnp.mean(xf * xf, axis=-1, keepdims=True) + EPS)
    return xf.astype(x.dtype) * g


def modulate(h, shift, scale):
    return h * (1 + scale) + shift


def axial_rope_tables(n, dtype):
    rows = n // GRID_W
    row_pos = jnp.repeat(jnp.arange(rows, dtype=jnp.float32), GRID_W)
    col_pos = jnp.tile(jnp.arange(GRID_W, dtype=jnp.float32), rows)
    inv_freq = ROPE_THETA ** (-jnp.arange(ROPE_PAIRS_PER_AXIS, dtype=jnp.float32) / ROPE_PAIRS_PER_AXIS)
    ang = jnp.concatenate([row_pos[:, None] * inv_freq, col_pos[:, None] * inv_freq], axis=-1)
    ang = ang[:, None, None, :]
    return jnp.cos(ang).astype(dtype), jnp.sin(ang).astype(dtype)


def apply_rope(t, cos, sin):
    half = ATT_QK_DIM // 2
    t1, t2 = t[..., :half], t[..., half:]
    return jnp.concatenate([t1 * cos - t2 * sin, t1 * sin + t2 * cos], axis=-1)


def split_projection(p):
    b, L = p.shape[:2]
    q, k, v, z, xbc, dt = jnp.split(p, IN_SPLITS, axis=-1)
    q = q.reshape(b, L, ATT_HEADS, 2, ATT_QK_DIM)
    k = k.reshape(b, L, ATT_HEADS, 2, ATT_QK_DIM)
    v = v.reshape(b, L, ATT_HEADS, ATT_V_DIM)
    dt = dt.reshape(b, L, 2, SSM_HEADS)
    return q, k, v, z, xbc, dt


def diff_attend(q, k, v, lam):
    s = jnp.einsum('bqhcd,bkhcd->bhcqk', q, k).astype(jnp.float32) * (ATT_QK_DIM ** -0.5)
    p = jax.nn.softmax(s, axis=-1)
    w = p[:, :, 0] - lam * p[:, :, 1]
    return jnp.einsum('bhqk,bkhd->bqhd', w.astype(v.dtype), v)


def differential_attention(q_l, k_l, v_l, q_c, k_c, v_c, cos, sin, lam, subln_g, lambda_init, need_ctx):
    b, n = q_l.shape[:2]
    q_l = apply_rope(q_l, cos, sin)
    k_l = apply_rope(k_l, cos, sin)
    k_all = jnp.concatenate([k_c, k_l], axis=1)
    v_all = jnp.concatenate([v_c, v_l], axis=1)
    n_blk = n // Q_BLOCK
    q_blocks = q_l.reshape(b, n_blk, Q_BLOCK, ATT_HEADS, 2, ATT_QK_DIM).swapaxes(0, 1)
    o_l = lax.map(lambda qb: diff_attend(qb, k_all, v_all, lam), q_blocks)
    o_l = o_l.swapaxes(0, 1).reshape(b, n, ATT_HEADS, ATT_V_DIM)

    def post(o):
        o = rmsnorm(o, subln_g) * (1 - lambda_init)
        return o.reshape(o.shape[:2] + (ATT_WIDTH,))

    o_c = post(diff_attend(q_c, k_c, v_c, lam)) if need_ctx else None
    return post(o_l), o_c


def dwconv_centred(u, w, bias):
    ch = u.shape[-1]
    pad = SSM_CONV // 2
    y = lax.conv_general_dilated(u, w[:, None, :].astype(u.dtype), window_strides=(1,),
                                 padding=[(pad, pad)], dimension_numbers=('NWC', 'WIO', 'NWC'),
                                 feature_group_count=ch)
    return y + bias


def ssd_scan(xs, dt, A, bm, cm, h0):
    b, L = xs.shape[:2]
    q = min(SSM_CHUNK, L)
    nc = L // q
    rs = lambda t: t.reshape((b, nc, q) + t.shape[2:])
    xs_c, dt_c, bm_c, cm_c = rs(xs), rs(dt), rs(bm), rs(cm)
    a_cs = jnp.cumsum(dt_c * A, axis=2)
    x_dt = xs_c * dt_c[..., None]
    mask = np.tril(np.ones((q, q), dtype=bool))[:, :, None, None]
    seg = a_cs[:, :, :, None] - a_cs[:, :, None, :]
    decay_in = jnp.exp(jnp.where(mask, seg, -jnp.inf))
    cb = jnp.einsum('bcqgn,bcsgn->bcqsg', cm_c, bm_c)
    y_diag = jnp.einsum('bcqsg,bcqsgr,bcsgrp->bcqgrp', cb, decay_in, x_dt)
    decay_to_end = jnp.exp(a_cs[:, :, -1:] - a_cs)
    states = jnp.einsum('bcqgn,bcqgr,bcqgrp->bcgrpn', bm_c, decay_to_end, x_dt)
    chunk_decay = jnp.exp(a_cs[:, :, -1])

    def step(h, inp):
        dec, st = inp
        return h * dec[..., None, None] + st, h

    h_final, h_prev = lax.scan(step, h0, (jnp.moveaxis(chunk_decay, 1, 0), jnp.moveaxis(states, 1, 0)))
    h_prev = jnp.moveaxis(h_prev, 0, 1)
    y_off = jnp.einsum('bcqgn,bcgrpn,bcqgr->bcqgrp', cm_c, h_prev, jnp.exp(a_cs))
    y = (y_diag + y_off).reshape(xs.shape)
    return y.astype(xs.dtype), h_final


def bidirectional_ssd(z_l, xbc_l, dt_l, z_c, xbc_c, dt_c, conv_w, conv_b, a_log, dt_bias, d_skip,
                      ssm_norm_g, need_ctx):
    A = -jnp.exp(a_log.astype(jnp.float32))
    A_f = A[0].reshape(SSM_GROUPS, SSM_HEADS_PER_GROUP)
    A_b = A[1].reshape(SSM_GROUPS, SSM_HEADS_PER_GROUP)
    flip = lambda t: jnp.flip(t, axis=1)

    def prep(xbc, dt):
        b, L = xbc.shape[:2]
        u = jax.nn.silu(dwconv_centred(xbc, conv_w, conv_b))
        xs, bm, cm = jnp.split(u, [SSM_WIDTH, SSM_WIDTH + SSM_GROUPS * SSM_STATE], axis=-1)
        xs = xs.reshape(b, L, SSM_GROUPS, SSM_HEADS_PER_GROUP, SSM_HEAD_DIM)
        bm = bm.reshape(b, L, SSM_GROUPS, SSM_STATE)
        cm = cm.reshape(b, L, SSM_GROUPS, SSM_STATE)
        dt = jax.nn.softplus((dt + dt_bias).astype(jnp.float32)).reshape(b, L, 2, SSM_GROUPS, SSM_HEADS_PER_GROUP)
        return xs, bm, cm, dt

    xs_c, b_c, c_c, dtc = prep(xbc_c, dt_c)
    xs_l, b_l, c_l, dtl = prep(xbc_l, dt_l)
    b = xs_c.shape[0]
    h0 = jnp.zeros((b, SSM_GROUPS, SSM_HEADS_PER_GROUP, SSM_HEAD_DIM, SSM_STATE), jnp.float32)
    yc_f, hc_f = ssd_scan(xs_c, dtc[:, :, 0], A_f, b_c, c_c, h0)
    yc_b, hc_b = ssd_scan(flip(xs_c), flip(dtc[:, :, 1]), A_b, flip(b_c), flip(c_c), h0)
    yl_f, _ = ssd_scan(xs_l, dtl[:, :, 0], A_f, b_l, c_l, hc_f)
    yl_b, _ = ssd_scan(flip(xs_l), flip(dtl[:, :, 1]), A_b, flip(b_l), flip(c_l), hc_b)
    d = d_skip.reshape(SSM_GROUPS, SSM_HEADS_PER_GROUP)[..., None]

    def finish(xs, y_f, y_b_rev, z):
        bb, L = xs.shape[:2]
        y = (y_f + flip(y_b_rev) + d * xs).reshape(bb, L, SSM_WIDTH) * jax.nn.silu(z)
        y = rmsnorm(y.reshape(bb, L, SSM_GROUPS, SSM_WIDTH // SSM_GROUPS),
                    ssm_norm_g.reshape(SSM_GROUPS, SSM_WIDTH // SSM_GROUPS))
        return y.reshape(bb, L, SSM_WIDTH)

    out_c = finish(xs_c, yc_f, yc_b, z_c) if need_ctx else None
    return finish(xs_l, yl_f, yl_b, z_l), out_c


def hybrid_mixer(h_l, h_c, w_in, w_out, lq1, lk1, lq2, lk2, subln_g, conv_w, conv_b, a_log, dt_bias,
                 d_skip, ssm_norm_g, cos, sin, lambda_init, need_ctx):
    q_l, k_l, v_l, z_l, xbc_l, dt_l = split_projection(h_l @ w_in)
    q_c, k_c, v_c, z_c, xbc_c, dt_c = split_projection(h_c @ w_in)
    f32 = jnp.float32
    lam = (jnp.exp(jnp.sum(lq1.astype(f32) * lk1.astype(f32)))
           - jnp.exp(jnp.sum(lq2.astype(f32) * lk2.astype(f32))) + lambda_init)
    att_l, att_c = differential_attention(q_l, k_l, v_l, q_c, k_c, v_c, cos, sin, lam, subln_g,
                                          lambda_init, need_ctx)
    ssm_l, ssm_c = bidirectional_ssd(z_l, xbc_l, dt_l, z_c, xbc_c, dt_c, conv_w, conv_b, a_log, dt_bias,
                                     d_skip, ssm_norm_g, need_ctx)
    out_l = jnp.concatenate([att_l, ssm_l], axis=-1) @ w_out
    out_c = jnp.concatenate([att_c, ssm_c], axis=-1) @ w_out if need_ctx else None
    return out_l, out_c


def swiglu(h, w1, w3, w2):
    return (jax.nn.silu(h @ w1) * (h @ w3)) @ w2


def moe_swiglu(h, router_w, w1, w3, w2):
    logits = (h @ router_w).astype(jnp.float32)
    top_v, top_i = lax.top_k(logits, TOP_K)
    gates = jax.nn.softmax(top_v, axis=-1)
    combine = jnp.sum(jax.nn.one_hot(top_i, N_EXPERTS, dtype=jnp.float32) * gates[..., None], axis=-2)
    out = jnp.zeros_like(h)
    for e in range(N_EXPERTS):
        out = out + combine[..., e:e + 1].astype(h.dtype) * swiglu(h, w1[e], w3[e], w2[e])
    return out


def setup_inputs(seed: int = 0) -> dict:
    key = jax.random.key(seed)
    ks = jax.random.split(key, 32)
    f32 = jnp.float32
    nrm = lambda k, shape, s: jax.random.normal(k, shape, f32) * s
    n_dense = (DEPTH + 1) // 2
    n_moe = DEPTH // 2
    dt_init = jnp.exp(jax.random.uniform(ks[15], (DEPTH, 2, SSM_HEADS), f32, math.log(1e-3), math.log(1e-1)))
    return {
        'x': nrm(ks[0], (BATCH, SEQ, D_MODEL), 1.0),
        'c': nrm(ks[1], (BATCH, D_MODEL), 1.0),
        'ctx': nrm(ks[2], (BATCH, CTX_LEN, D_MODEL), 1.0),
        'c_ctx': nrm(ks[3], (D_MODEL,), 1.0),
        'w_mod': nrm(ks[4], (DEPTH, D_MODEL, N_MOD * D_MODEL), 0.5 * D_MODEL ** -0.5),
        'b_mod': nrm(ks[5], (DEPTH, N_MOD * D_MODEL), 0.02),
        'norm_mix_g': 1.0 + nrm(ks[6], (DEPTH, D_MODEL), 0.1),
        'w_in': nrm(ks[7], (DEPTH, D_MODEL, IN_COLS), D_MODEL ** -0.5),
        'lambda_q1': nrm(ks[8], (DEPTH, ATT_QK_DIM), 0.1),
        'lambda_k1': nrm(ks[9], (DEPTH, ATT_QK_DIM), 0.1),
        'lambda_q2': nrm(ks[10], (DEPTH, ATT_QK_DIM), 0.1),
        'lambda_k2': nrm(ks[11], (DEPTH, ATT_QK_DIM), 0.1),
        'subln_g': 1.0 + nrm(ks[12], (DEPTH, ATT_V_DIM), 0.1),
        'conv_w': nrm(ks[13], (DEPTH, SSM_CONV, XBC_COLS), SSM_CONV ** -0.5),
        'conv_b': nrm(ks[14], (DEPTH, XBC_COLS), 0.02),
        'a_log': jnp.log(jax.random.uniform(ks[16], (DEPTH, 2, SSM_HEADS), f32, 1.0, 16.0)),
        'dt_bias': dt_init + jnp.log(-jnp.expm1(-dt_init)),
        'd_skip': 1.0 + nrm(ks[17], (DEPTH, SSM_HEADS), 0.1),
        'ssm_norm_g': 1.0 + nrm(ks[18], (DEPTH, SSM_WIDTH), 0.1),
        'w_out': nrm(ks[19], (DEPTH, MIX_WIDTH, D_MODEL), MIX_WIDTH ** -0.5),
        'norm_ffn_g': 1.0 + nrm(ks[20], (DEPTH, D_MODEL), 0.1),
        'ffn_w1': nrm(ks[21], (n_dense, D_MODEL, D_FF), D_MODEL ** -0.5),
        'ffn_w3': nrm(ks[22], (n_dense, D_MODEL, D_FF), D_MODEL ** -0.5),
        'ffn_w2': nrm(ks[23], (n_dense, D_FF, D_MODEL), D_FF ** -0.5),
        'router_w': nrm(ks[24], (n_moe, D_MODEL, N_EXPERTS), D_MODEL ** -0.5),
        'moe_w1': nrm(ks[25], (n_moe, N_EXPERTS, D_MODEL, D_FF_EXPERT), D_MODEL ** -0.5),
        'moe_w3': nrm(ks[26], (n_moe, N_EXPERTS, D_MODEL, D_FF_EXPERT), D_MODEL ** -0.5),
        'moe_w2': nrm(ks[27], (n_moe, N_EXPERTS, D_FF_EXPERT, D_MODEL), D_FF_EXPERT ** -0.5),
        'final_g': 1.0 + nrm(ks[28], (D_MODEL,), 0.1),
    }


def reference(x, c, ctx, c_ctx, w_mod, b_mod, norm_mix_g, w_in, lambda_q1, lambda_k1, lambda_q2, lambda_k2,
              subln_g, conv_w, conv_b, a_log, dt_bias, d_skip, ssm_norm_g, w_out, norm_ffn_g,
              ffn_w1, ffn_w3, ffn_w2, router_w, moe_w1, moe_w3, moe_w2, final_g):
    n = x.shape[1]
    cos, sin = axial_rope_tables(n, x.dtype)
    silu_c = jax.nn.silu(c)
    silu_cc = jax.nn.silu(c_ctx)
    for li in range(DEPTH):
        need_ctx = li < DEPTH - 1
        lambda_init = 0.8 - 0.6 * math.exp(-0.3 * li)
        mod_l = jnp.split((silu_c @ w_mod[li] + b_mod[li])[:, None, :], N_MOD, axis=-1)
        mod_c = jnp.split(silu_cc @ w_mod[li] + b_mod[li], N_MOD, axis=-1)

        h_l = modulate(rmsnorm(x, norm_mix_g[li]), mod_l[0], mod_l[1])
        h_c = modulate(rmsnorm(ctx, norm_mix_g[li]), mod_c[0], mod_c[1])
        mix_l, mix_c = hybrid_mixer(h_l, h_c, w_in[li], w_out[li], lambda_q1[li], lambda_k1[li],
                                    lambda_q2[li], lambda_k2[li], subln_g[li], conv_w[li], conv_b[li],
                                    a_log[li], dt_bias[li], d_skip[li], ssm_norm_g[li], cos, sin,
                                    lambda_init, need_ctx)
        x = x + mod_l[2] * mix_l
        if need_ctx:
            ctx = ctx + mod_c[2] * mix_c

        j = li // 2
        if li % 2 == 0:
            def channel_mixer(h):
                return swiglu(h, ffn_w1[j], ffn_w3[j], ffn_w2[j])
        else:
            def channel_mixer(h):
                return moe_swiglu(h, router_w[j], moe_w1[j], moe_w3[j], moe_w2[j])
        h_l = modulate(rmsnorm(x, norm_ffn_g[li]), mod_l[3], mod_l[4])
        x = x + mod_l[5] * channel_mixer(h_l)
        if need_ctx:
            h_c = modulate(rmsnorm(ctx, norm_ffn_g[li]), mod_c[3], mod_c[4])
            ctx = ctx + mod_c[5] * channel_mixer(h_c)
    return rmsnorm(x, final_g)
```

```python
import functools
import math

import jax
import jax.numpy as jnp
from jax import lax
from jax.experimental import pallas as pl
from jax.experimental.pallas import tpu as pltpu

F32 = jnp.float32
BF16 = jnp.bfloat16

GRID_W = 64
EPS = 1e-6
N_MOD = 6
ATT_HEADS = 8
ATT_QK_DIM = 64
ATT_V_DIM = 128
ATT_WIDTH = ATT_HEADS * ATT_V_DIM
ROPE_THETA = 10000.0
ROPE_PAIRS_PER_AXIS = ATT_QK_DIM // 4
SSM_HEAD_DIM = 64
SSM_GROUPS = 2
SSM_HEADS_PER_GROUP = 8
SSM_HEADS = SSM_GROUPS * SSM_HEADS_PER_GROUP
SSM_WIDTH = SSM_HEADS * SSM_HEAD_DIM
SSM_STATE = 128
SSM_CONV = 5
SSM_CHUNK = 128
GROUP_WIDTH = SSM_WIDTH // SSM_GROUPS
XBC_COLS = SSM_WIDTH + 2 * SSM_GROUPS * SSM_STATE
TOP_K = 2

LANES = 128
SUB = 256
VMEM_LIMIT = 52 * 1024 * 1024

COL_Q = 0
COL_K = COL_Q + ATT_WIDTH
COL_V = COL_K + ATT_WIDTH
COL_Z = COL_V + ATT_WIDTH
COL_XBC = COL_Z + SSM_WIDTH
COL_DT = COL_XBC + XBC_COLS
IN_COLS_PAD = COL_DT + 2 * SSM_GROUPS * LANES


def _pick(n, cands):
    for c in cands:
        if n % c == 0:
            return c
    raise ValueError(f"no tile in {cands} divides {n}")


def _params(sem, vmem=VMEM_LIMIT, **kw):
    return pltpu.CompilerParams(dimension_semantics=sem, vmem_limit_bytes=vmem, **kw)


def _mod_row(sub_idx, geo):
    subs_per_batch, ctx_subs, n_batch = geo
    b = sub_idx // subs_per_batch
    w = sub_idx % subs_per_batch
    return jnp.where(w < ctx_subs, n_batch, b)


def _normmod(x, g, shift, scale):
    ms = jnp.mean(x * x, axis=-1, keepdims=True)
    hn = (x * lax.rsqrt(ms + EPS)) * g
    return hn * (1.0 + scale) + shift


def _silu(x):
    return x * (1.0 / (1.0 + jnp.exp(-x)))


def _mod_kernel(c_ref, w_ref, b_ref, o_ref):
    s = _silu(c_ref[...]).astype(BF16)
    o_ref[...] = jnp.dot(s, w_ref[...].astype(BF16), preferred_element_type=F32) + b_ref[...]


def _mod_table(c_all, w_mod, b_mod):
    depth, d, n = w_mod.shape
    tn = _pick(n, (1024, 512, 256, 128))
    return pl.pallas_call(
        _mod_kernel,
        grid=(depth, n // tn),
        in_specs=[
            pl.BlockSpec((8, d), lambda l, j: (0, 0)),
            pl.BlockSpec((None, d, tn), lambda l, j: (l, 0, j)),
            pl.BlockSpec((None, 1, tn), lambda l, j: (l, 0, j)),
        ],
        out_specs=pl.BlockSpec((None, 8, tn), lambda l, j: (l, 0, j)),
        out_shape=jax.ShapeDtypeStruct((depth, 8, n), F32),
        compiler_params=_params(("parallel", "parallel")),
        name="mod_table",
    )(c_all, w_mod, b_mod.reshape(depth, 1, n))


def _in_proj_kernel(x_ref, g_ref, mod_ref, w_ref, cos_ref, sa_ref, sb_ref, o_ref, h_ref, *, tm, tn, d, geo):
    i = pl.program_id(0)
    j = pl.program_id(1)

    @pl.when(j == 0)
    def _():
        for u in range(tm // SUB):
            row = _mod_row(i * (tm // SUB) + u, geo)
            rs = slice(u * SUB, (u + 1) * SUB)
            shift = mod_ref[pl.ds(row, 1), 0:d]
            scale = mod_ref[pl.ds(row, 1), d:2 * d]
            h_ref[rs, :] = _normmod(x_ref[rs, :], g_ref[...], shift, scale).astype(BF16)

    acc = jnp.dot(h_ref[...], w_ref[...], preferred_element_type=F32)

    def rope(scale):
        outs = []
        for hs in range(tn // LANES):
            a = acc[:, hs * LANES:(hs + 1) * LANES]
            r = (a * cos_ref[...] + pltpu.roll(a, LANES - 32, 1) * sa_ref[...]
                 + pltpu.roll(a, 32, 1) * sb_ref[...])
            outs.append(r * scale)
        return jnp.concatenate(outs, axis=1)

    @pl.when(j == COL_Q // tn)
    def _():
        o_ref[...] = rope(ATT_QK_DIM ** -0.5).astype(BF16)

    @pl.when(j == COL_K // tn)
    def _():
        o_ref[...] = rope(1.0).astype(BF16)

    @pl.when(j >= COL_V // tn)
    def _():
        o_ref[...] = acc.astype(BF16)


def _in_proj(x, g, mod, w, cos, sa, sb, geo):
    m, d = x.shape
    n = w.shape[1]
    tm = _pick(m, (1024, 512, 256))
    tn = ATT_WIDTH
    kern = functools.partial(_in_proj_kernel, tm=tm, tn=tn, d=d, geo=geo)
    return pl.pallas_call(
        kern,
        grid=(m // tm, n // tn),
        in_specs=[
            pl.BlockSpec((tm, d), lambda i, j: (i, 0)),
            pl.BlockSpec((1, d), lambda i, j: (0, 0)),
            pl.BlockSpec(mod.shape, lambda i, j: (0, 0)),
            pl.BlockSpec((d, tn), lambda i, j: (0, j)),
            pl.BlockSpec((tm, LANES), lambda i, j: (i, 0)),
            pl.BlockSpec((tm, LANES), lambda i, j: (i, 0)),
            pl.BlockSpec((tm, LANES), lambda i, j: (i, 0)),
        ],
        out_specs=pl.BlockSpec((tm, tn), lambda i, j: (i, j)),
        out_shape=jax.ShapeDtypeStruct((m, n), BF16),
        scratch_shapes=[pltpu.VMEM((tm, d), BF16)],
        compiler_params=_params(("parallel", "arbitrary")),
        name="in_proj",
    )(x, g.reshape(1, d), mod, w, cos, sa, sb)


def _attn_kernel(lam_ref, q_ref, k_ref, v_ref, g_ref, o_ref, *, tq, tk, ctx_tiles, ctx_chunks, all_chunks,
                 out_scale):
    qi = pl.program_id(2)
    q = q_ref[...]
    lane = lax.broadcasted_iota(jnp.int32, (tq, LANES), 1)
    zero = jnp.zeros_like(q)
    q2 = jnp.concatenate([jnp.where(lane < ATT_QK_DIM, q, zero), jnp.where(lane >= ATT_QK_DIM, q, zero)], axis=0)
    n_chunks = jnp.where(qi < ctx_tiles, ctx_chunks, all_chunks)

    def body(c, carry):
        m, l, acc = carry
        off = pl.multiple_of(c * tk, tk)
        kc = k_ref[pl.ds(off, tk), :]
        vc = v_ref[pl.ds(off, tk), :]
        s = lax.dot_general(q2, kc, (((1,), (1,)), ((), ())), preferred_element_type=F32)
        m_new = jnp.maximum(m, jnp.max(s, axis=-1, keepdims=True))
        p = jnp.exp(s - m_new)
        alpha = jnp.exp(m - m_new)
        l = alpha * l + jnp.sum(p, axis=-1, keepdims=True)
        acc = alpha * acc + jnp.dot(p.astype(BF16), vc, preferred_element_type=F32)
        return m_new, l, acc

    m0 = jnp.full((2 * tq, 1), -1e30, F32)
    l0 = jnp.zeros((2 * tq, 1), F32)
    a0 = jnp.zeros((2 * tq, LANES), F32)
    _, l, acc = lax.fori_loop(0, n_chunks, body, (m0, l0, a0))
    o = acc * (1.0 / l)
    o = o[:tq] - lam_ref[0] * o[tq:]
    ms = jnp.mean(o * o, axis=-1, keepdims=True)
    o = (o * lax.rsqrt(ms + EPS)) * g_ref[...] * out_scale
    o_ref[...] = o.astype(BF16)


def _attention(p3, lam, subln_g, lambda_init, ctx_len):
    b, t, _ = p3.shape
    tq = SUB
    tk = SUB
    kern = functools.partial(_attn_kernel, tq=tq, tk=tk, ctx_tiles=ctx_len // tq, ctx_chunks=ctx_len // tk,
                             all_chunks=t // tk, out_scale=1.0 - lambda_init)
    return pl.pallas_call(
        kern,
        grid=(b, ATT_HEADS, t // tq),
        in_specs=[
            pl.BlockSpec(memory_space=pltpu.SMEM),
            pl.BlockSpec((None, tq, LANES), lambda bi, h, qi: (bi, qi, COL_Q // LANES + h)),
            pl.BlockSpec((None, t, LANES), lambda bi, h, qi: (bi, 0, COL_K // LANES + h)),
            pl.BlockSpec((None, t, LANES), lambda bi, h, qi: (bi, 0, COL_V // LANES + h)),
            pl.BlockSpec((1, LANES), lambda bi, h, qi: (0, 0)),
        ],
        out_specs=pl.BlockSpec((None, tq, LANES), lambda bi, h, qi: (bi, qi, h)),
        out_shape=jax.ShapeDtypeStruct((b, t, ATT_WIDTH), BF16),
        compiler_params=_params(("parallel", "parallel", "arbitrary")),
        name="diff_attention",
    )(lam, p3, p3, p3, subln_g.reshape(1, LANES))


CONV_HALO = 16


def _conv_kernel(prev_ref, cur_ref, next_ref, w_ref, b_ref, o_ref, *, rb, ctx_blocks, n_blocks):
    i = pl.program_id(1)
    prev_ok = jnp.logical_and(i != 0, i != ctx_blocks)
    next_ok = jnp.logical_and(i != ctx_blocks - 1, i != n_blocks - 1)
    prev = jnp.where(prev_ok, prev_ref[...].astype(F32), 0.0)
    nxt = jnp.where(next_ok, next_ref[...].astype(F32), 0.0)
    xc = jnp.concatenate([prev, cur_ref[...].astype(F32), nxt], axis=0)
    pad = SSM_CONV // 2
    acc = b_ref[...] + w_ref[0:1, :] * xc[CONV_HALO - pad:CONV_HALO - pad + rb, :]
    for k in range(1, SSM_CONV):
        s = CONV_HALO - pad + k
        acc = acc + w_ref[k:k + 1, :] * xc[s:s + rb, :]
    o_ref[...] = _silu(acc).astype(BF16)


def _conv_silu(p3, conv_w, conv_b, ctx_len):
    b, t, _ = p3.shape
    rb = SUB
    tc = 512
    hb = rb // CONV_HALO
    n_blocks = t // rb
    c0 = COL_XBC // tc
    kern = functools.partial(_conv_kernel, rb=rb, ctx_blocks=ctx_len // rb, n_blocks=n_blocks)
    last_halo = t // CONV_HALO - 1
    return pl.pallas_call(
        kern,
        grid=(b, n_blocks, XBC_COLS // tc),
        in_specs=[
            pl.BlockSpec((None, CONV_HALO, tc), lambda bi, i, j: (bi, jnp.maximum(i * hb - 1, 0), c0 + j)),
            pl.BlockSpec((None, rb, tc), lambda bi, i, j: (bi, i, c0 + j)),
            pl.BlockSpec((None, CONV_HALO, tc), lambda bi, i, j: (bi, jnp.minimum((i + 1) * hb, last_halo), c0 + j)),
            pl.BlockSpec((SSM_CONV, tc), lambda bi, i, j: (0, j)),
            pl.BlockSpec((1, tc), lambda bi, i, j: (0, j)),
        ],
        out_specs=pl.BlockSpec((None, rb, tc), lambda bi, i, j: (bi, i, j)),
        out_shape=jax.ShapeDtypeStruct((b, t, XBC_COLS), BF16),
        compiler_params=_params(("parallel", "parallel", "parallel")),
        name="conv_silu",
    )(p3, p3, p3, conv_w, conv_b.reshape(1, XBC_COLS))


def _split_bf16(a, pieces):
    parts = []
    r = a
    for _ in range(pieces):
        p = r.astype(BF16)
        parts.append(p)
        r = r - p.astype(F32)
    return parts


def _dot_sel(a, sel, pieces):
    out = None
    for p in _split_bf16(a, pieces):
        t = jnp.dot(p, sel, preferred_element_type=F32)
        out = t if out is None else out + t
    return out


def _sel_dot(sel, a, pieces):
    out = None
    for p in _split_bf16(a, pieces):
        t = jnp.dot(sel, p, preferred_element_type=F32)
        out = t if out is None else out + t
    return out


def _ssd_chunk(xs, bm, cm, dtb, bias, alog, h_ref, tri, ex64, ex128, reverse):
    q = xs.shape[0]
    hp = SSM_HEADS_PER_GROUP
    dt = jax.nn.softplus(dtb.astype(F32) + bias)
    a = dt * (-jnp.exp(alog))
    acs = _sel_dot(tri, a, 3)
    acs_t = acs.T
    end = acs[0:1, :] if reverse else acs[q - 1:q, :]
    dt_e = _dot_sel(dt, ex64, 2)
    eacs_e = _dot_sel(jnp.exp(acs), ex64, 2)
    dte_e = _dot_sel(jnp.exp(end - acs), ex64, 2)
    colb = _dot_sel(acs, ex128, 3)
    xsf = xs.astype(F32)
    xdt = xsf * dt_e
    cb = lax.dot_general(cm, bm, (((1,), (1,)), ((), ())), preferred_element_type=F32)
    qi = lax.broadcasted_iota(jnp.int32, (q, q), 0)
    si = lax.broadcasted_iota(jnp.int32, (q, q), 1)
    mask = (si >= qi) if reverse else (si <= qi)
    lane = lax.broadcasted_iota(jnp.int32, (q, LANES), 1)
    ys = []
    for j in range(hp // 2):
        xp = xdt[:, j * LANES:(j + 1) * LANES]
        halves = (jnp.where(lane < SSM_HEAD_DIM, xp, 0.0).astype(BF16),
                  jnp.where(lane >= SSM_HEAD_DIM, xp, 0.0).astype(BF16))
        acc = None
        for k in range(2):
            r = 2 * j + k
            seg = colb[:, r * LANES:(r + 1) * LANES] - acs_t[r:r + 1, :]
            dec = jnp.exp(jnp.where(mask, seg, -1e30))
            t = jnp.dot((cb * dec).astype(BF16), halves[k], preferred_element_type=F32)
            acc = t if acc is None else acc + t
        ys.append(acc)
    y_diag = jnp.concatenate(ys, axis=1)
    h = h_ref[...]
    y_off = jnp.dot(cm, h.astype(BF16), preferred_element_type=F32) * eacs_e
    cdec = eacs_e[0:1, :] if reverse else eacs_e[q - 1:q, :]
    xd = (xdt * dte_e).astype(BF16)
    h_ref[...] = h * cdec + lax.dot_general(bm, xd, (((0,), (0,)), ((), ())), preferred_element_type=F32)
    return y_diag + y_off, xsf


def _ssd_kernel(xsf_ref, bmf_ref, cmf_ref, dtf_ref, xsb_ref, bmb_ref, cmb_ref, dtb_ref,
                biasf_ref, biasb_ref, alogf_ref, alogb_ref, dskip_ref, tril_ref, triu_ref, ex64_ref, ex128_ref,
                yf_ref, yb_ref, hf_ref, hb_ref):
    @pl.when(pl.program_id(2) == 0)
    def _():
        hf_ref[...] = jnp.zeros_like(hf_ref)
        hb_ref[...] = jnp.zeros_like(hb_ref)

    yf, xsf = _ssd_chunk(xsf_ref[...], bmf_ref[...], cmf_ref[...], dtf_ref[...], biasf_ref[...], alogf_ref[...],
                         hf_ref, tril_ref[...], ex64_ref[...], ex128_ref[...], False)
    yf_ref[...] = yf + dskip_ref[...] * xsf
    yb, _ = _ssd_chunk(xsb_ref[...], bmb_ref[...], cmb_ref[...], dtb_ref[...], biasb_ref[...], alogb_ref[...],
                       hb_ref, triu_ref[...], ex64_ref[...], ex128_ref[...], True)
    yb_ref[...] = yb


def _ssd_scan(u3, p3, bias4, alog4, dskip, ctx_len):
    b, t, _ = u3.shape
    q = SSM_CHUNK
    ns = t // q
    ncc = ctx_len // q
    g_n = SSM_GROUPS

    def fwd(s):
        return s

    def bwd(s):
        return jnp.where(s < ncc, ncc - 1 - s, ns - 1 + ncc - s)

    xs_blk = GROUP_WIDTH // LANES
    b_col = SSM_WIDTH // LANES
    c_col = b_col + g_n
    dt_col = COL_DT // LANES
    del xs_blk

    def specs(cidx, direction):
        return [
            pl.BlockSpec((None, q, GROUP_WIDTH), lambda bi, g, s: (bi, cidx(s), g)),
            pl.BlockSpec((None, q, LANES), lambda bi, g, s: (bi, cidx(s), b_col + g)),
            pl.BlockSpec((None, q, LANES), lambda bi, g, s: (bi, cidx(s), c_col + g)),
            pl.BlockSpec((None, q, LANES), lambda bi, g, s: (bi, cidx(s), dt_col + direction * g_n + g)),
        ]

    row = lambda direction: pl.BlockSpec((None, 1, LANES), lambda bi, g, s: (direction * g_n + g, 0, 0))
    full = lambda arr: pl.BlockSpec(arr.shape, lambda bi, g, s: (0,) * arr.ndim)

    ri = lax.broadcasted_iota(jnp.int32, (q, q), 0)
    ci = lax.broadcasted_iota(jnp.int32, (q, q), 1)
    tril = (ci <= ri).astype(BF16)
    triu = (ci >= ri).astype(BF16)
    r64 = lax.broadcasted_iota(jnp.int32, (LANES, GROUP_WIDTH), 0)
    c64 = lax.broadcasted_iota(jnp.int32, (LANES, GROUP_WIDTH), 1)
    ex64 = (c64 // SSM_HEAD_DIM == r64).astype(BF16)
    r128 = lax.broadcasted_iota(jnp.int32, (LANES, SSM_HEADS_PER_GROUP * LANES), 0)
    c128 = lax.broadcasted_iota(jnp.int32, (LANES, SSM_HEADS_PER_GROUP * LANES), 1)
    ex128 = (c128 // LANES == r128).astype(BF16)

    in_specs = (specs(fwd, 0) + specs(bwd, 1) + [row(0), row(1), row(0), row(1),
                pl.BlockSpec((None, 1, GROUP_WIDTH), lambda bi, g, s: (g, 0, 0)),
                full(tril), full(triu), full(ex64), full(ex128)])
    out_spec = lambda cidx: pl.BlockSpec((None, q, GROUP_WIDTH), lambda bi, g, s: (bi, cidx(s), g))
    return pl.pallas_call(
        _ssd_kernel,
        grid=(b, g_n, ns),
        in_specs=in_specs,
        out_specs=[out_spec(fwd), out_spec(bwd)],
        out_shape=[jax.ShapeDtypeStruct((b, t, SSM_WIDTH), F32)] * 2,
        scratch_shapes=[pltpu.VMEM((SSM_STATE, GROUP_WIDTH), F32)] * 2,
        compiler_params=_params(("parallel", "parallel", "arbitrary")),
        name="ssd_scan",
    )(u3, u3, u3, p3, u3, u3, u3, p3, bias4, bias4, alog4, alog4, dskip, tril, triu, ex64, ex128)


def _ssd_finish_kernel(yf_ref, yb_ref, z_ref, g_ref, o_ref):
    y = (yf_ref[...] + yb_ref[...]) * _silu(z_ref[...].astype(F32))
    outs = []
    for g in range(SSM_GROUPS):
        yg = y[:, g * GROUP_WIDTH:(g + 1) * GROUP_WIDTH]
        ms = jnp.mean(yg * yg, axis=-1, keepdims=True)
        outs.append((yg * lax.rsqrt(ms + EPS)) * g_ref[:, g * GROUP_WIDTH:(g + 1) * GROUP_WIDTH])
    o_ref[...] = jnp.concatenate(outs, axis=1).astype(BF16)


def _ssd_finish(yf, yb, p, norm_g):
    m = yf.shape[0]
    tm = _pick(m, (512, 256))
    return pl.pallas_call(
        _ssd_finish_kernel,
        grid=(m // tm,),
        in_specs=[
            pl.BlockSpec((tm, SSM_WIDTH), lambda i: (i, 0)),
            pl.BlockSpec((tm, SSM_WIDTH), lambda i: (i, 0)),
            pl.BlockSpec((tm, SSM_WIDTH), lambda i: (i, COL_Z // SSM_WIDTH)),
            pl.BlockSpec((1, SSM_WIDTH), lambda i: (0, 0)),
        ],
        out_specs=pl.BlockSpec((tm, SSM_WIDTH), lambda i: (i, 0)),
        out_shape=jax.ShapeDtypeStruct((m, SSM_WIDTH), BF16),
        compiler_params=_params(("parallel",)),
        name="ssd_finish",
    )(yf, yb, p, norm_g.reshape(1, SSM_WIDTH))


def _mm_resid_kernel(*refs, n_a, tm, geo):
    a_refs = refs[:n_a]
    w_refs = refs[n_a:2 * n_a]
    x_ref, gate_ref, o_ref = refs[2 * n_a:]
    i = pl.program_id(0)
    acc = None
    for a_ref, w_ref in zip(a_refs, w_refs):
        t = jnp.dot(a_ref[...], w_ref[...], preferred_element_type=F32)
        acc = t if acc is None else acc + t
    for u in range(tm // SUB):
        row = _mod_row(i * (tm // SUB) + u, geo)
        rs = slice(u * SUB, (u + 1) * SUB)
        o_ref[rs, :] = x_ref[rs, :] + gate_ref[pl.ds(row, 1), :] * acc[rs, :]


def _mm_resid(a_list, w_list, x, mod, gate_idx, geo, tn):
    m, d = x.shape
    tm = _pick(m, (1024, 512, 256))
    n_a = len(a_list)
    kern = functools.partial(_mm_resid_kernel, n_a=n_a, tm=tm, geo=geo)
    gate_blk = gate_idx * d // tn
    in_specs = ([pl.BlockSpec((tm, a.shape[1]), lambda i, j: (i, 0)) for a in a_list]
                + [pl.BlockSpec((w.shape[0], tn), lambda i, j: (0, j)) for w in w_list]
                + [pl.BlockSpec((tm, tn), lambda i, j: (i, j)),
                   pl.BlockSpec((mod.shape[0], tn), lambda i, j: (0, gate_blk + j))])
    return pl.pallas_call(
        kern,
        grid=(m // tm, d // tn),
        in_specs=in_specs,
        out_specs=pl.BlockSpec((tm, tn), lambda i, j: (i, j)),
        out_shape=jax.ShapeDtypeStruct((m, d), F32),
        compiler_params=_params(("parallel", "parallel")),
        name="matmul_gated_residual",
    )(*a_list, *w_list, x, mod)


def _ffn_in_kernel(x_ref, g_ref, mod_ref, w1_ref, w3_ref, o_ref, h_ref, *, tm, d, geo):
    i = pl.program_id(0)

    @pl.when(pl.program_id(1) == 0)
    def _():
        for u in range(tm // SUB):
            row = _mod_row(i * (tm // SUB) + u, geo)
            rs = slice(u * SUB, (u + 1) * SUB)
            shift = mod_ref[pl.ds(row, 1), 3 * d:4 * d]
            scale = mod_ref[pl.ds(row, 1), 4 * d:5 * d]
            h_ref[rs, :] = _normmod(x_ref[rs, :], g_ref[...], shift, scale).astype(BF16)

    h = h_ref[...]
    a = jnp.dot(h, w1_ref[...], preferred_element_type=F32)
    b = jnp.dot(h, w3_ref[...], preferred_element_type=F32)
    o_ref[...] = (_silu(a) * b).astype(BF16)


def _ffn_in(x, g, mod, w1, w3, geo):
    m, d = x.shape
    ff = w1.shape[1]
    tm = _pick(m, (1024, 512, 256))
    tf = _pick(ff, (512, 256, 128))
    kern = functools.partial(_ffn_in_kernel, tm=tm, d=d, geo=geo)
    return pl.pallas_call(
        kern,
        grid=(m // tm, ff // tf),
        in_specs=[
            pl.BlockSpec((tm, d), lambda i, j: (i, 0)),
            pl.BlockSpec((1, d), lambda i, j: (0, 0)),
            pl.BlockSpec(mod.shape, lambda i, j: (0, 0)),
            pl.BlockSpec((d, tf), lambda i, j: (0, j)),
            pl.BlockSpec((d, tf), lambda i, j: (0, j)),
        ],
        out_specs=pl.BlockSpec((tm, tf), lambda i, j: (i, j)),
        out_shape=jax.ShapeDtypeStruct((m, ff), BF16),
        scratch_shapes=[pltpu.VMEM((tm, d), BF16)],
        compiler_params=_params(("parallel", "arbitrary")),
        name="ffn_in",
    )(x, g.reshape(1, d), mod, w1, w3)


def _router_kernel(x_ref, g_ref, mod_ref, rw_ref, h_ref, r_ref, *, tm, d, n_exp, geo):
    i = pl.program_id(0)
    for u in range(tm // SUB):
        row = _mod_row(i * (tm // SUB) + u, geo)
        rs = slice(u * SUB, (u + 1) * SUB)
        shift = mod_ref[pl.ds(row, 1), 3 * d:4 * d]
        scale = mod_ref[pl.ds(row, 1), 4 * d:5 * d]
        h_ref[rs, :] = _normmod(x_ref[rs, :], g_ref[...], shift, scale)
    logits = jnp.dot(h_ref[...].astype(BF16), rw_ref[...], preferred_element_type=F32)
    lane = lax.broadcasted_iota(jnp.int32, logits.shape, 1)
    neg = jnp.float32(-1e30)
    logits = jnp.where(lane < n_exp, logits, neg)
    m1 = jnp.max(logits, axis=-1, keepdims=True)
    i1 = jnp.min(jnp.where(logits == m1, lane, LANES), axis=-1, keepdims=True)
    rest = jnp.where(lane == i1, neg, logits)
    m2 = jnp.max(rest, axis=-1, keepdims=True)
    i2 = jnp.min(jnp.where(rest == m2, lane, LANES), axis=-1, keepdims=True)
    t = jnp.exp(m2 - m1)
    g1 = 1.0 / (1.0 + t)
    g2 = t * g1
    out = jnp.where(lane == 0, i1.astype(F32),
                    jnp.where(lane == 1, i2.astype(F32),
                              jnp.where(lane == 2, g1, jnp.where(lane == 3, g2, 0.0))))
    r_ref[...] = out


def _router(x, g, mod, rw_pad, n_exp, geo):
    m, d = x.shape
    tm = _pick(m, (512, 256))
    kern = functools.partial(_router_kernel, tm=tm, d=d, n_exp=n_exp, geo=geo)
    return pl.pallas_call(
        kern,
        grid=(m // tm,),
        in_specs=[
            pl.BlockSpec((tm, d), lambda i: (i, 0)),
            pl.BlockSpec((1, d), lambda i: (0, 0)),
            pl.BlockSpec(mod.shape, lambda i: (0, 0)),
            pl.BlockSpec((d, LANES), lambda i: (0, 0)),
        ],
        out_specs=[pl.BlockSpec((tm, d), lambda i: (i, 0)), pl.BlockSpec((tm, LANES), lambda i: (i, 0))],
        out_shape=[jax.ShapeDtypeStruct((m, d), F32), jax.ShapeDtypeStruct((m, LANES), F32)],
        compiler_params=_params(("parallel",)),
        name="moe_router",
    )(x, g.reshape(1, d), mod, rw_pad)


GATHER_ROWS = 1024


def _gather_kernel(*refs, n_out, rows):
    idx_refs = refs[:n_out]
    src_ref = refs[n_out]
    dst_refs = refs[n_out + 1:2 * n_out + 1]
    sem = refs[2 * n_out + 1]
    base = pl.program_id(0) * rows

    def copy(k, r):
        return pltpu.make_async_copy(src_ref.at[pl.ds(idx_refs[k][r], 1), :],
                                     dst_refs[k].at[pl.ds(base + r, 1), :], sem)

    def start(r, c):
        for k in range(n_out):
            copy(k, r).start()
        return c

    def wait(r, c):
        for k in range(n_out):
            copy(k, r).wait()
        return c

    lax.fori_loop(0, rows, start, 0)
    lax.fori_loop(0, rows, wait, 0)


def _gather_rows(src, idx_list):
    n = idx_list[0].shape[0]
    rows = GATHER_ROWS
    assert n % rows == 0
    n_out = len(idx_list)
    kern = functools.partial(_gather_kernel, n_out=n_out, rows=rows)
    outs = pl.pallas_call(
        kern,
        grid=(n // rows,),
        in_specs=([pl.BlockSpec((rows,), lambda i: (i,), memory_space=pltpu.SMEM) for _ in idx_list]
                  + [pl.BlockSpec(memory_space=pl.ANY)]),
        out_specs=[pl.BlockSpec(memory_space=pl.ANY) for _ in idx_list],
        out_shape=[jax.ShapeDtypeStruct((n, src.shape[1]), src.dtype) for _ in idx_list],
        scratch_shapes=[pltpu.SemaphoreType.DMA(())],
        compiler_params=_params(("arbitrary",), has_side_effects=True),
        name="gather_rows",
    )(*idx_list, src)
    return outs


def _moe_in_kernel(te_ref, nu_ref, h_ref, w1_ref, w3_ref, o_ref, hb_ref):
    t = pl.program_id(0)
    j = pl.program_id(1)

    @pl.when(t < nu_ref[0])
    def _():
        @pl.when(j == 0)
        def _():
            hb_ref[...] = h_ref[...].astype(BF16)

        h = hb_ref[...]
        a = jnp.dot(h, w1_ref[...], preferred_element_type=F32)
        b = jnp.dot(h, w3_ref[...], preferred_element_type=F32)
        o_ref[...] = (_silu(a) * b).astype(BF16)

    @pl.when(t >= nu_ref[0])
    def _():
        o_ref[...] = jnp.zeros_like(o_ref)


def _moe_in(hs, w1, w3, tile_expert, n_used, tm):
    s, d = hs.shape
    ff = w1.shape[2]
    tf = _pick(ff, (512, 256, 128))
    n_ff = ff // tf

    def w_map(t, j, te, nu):
        return (te[t], 0, jnp.where(t < nu[0], j, n_ff - 1))

    return pl.pallas_call(
        _moe_in_kernel,
        grid_spec=pltpu.PrefetchScalarGridSpec(
            num_scalar_prefetch=2,
            grid=(s // tm, n_ff),
            in_specs=[
                pl.BlockSpec((tm, d), lambda t, j, te, nu: (t, 0)),
                pl.BlockSpec((None, d, tf), w_map),
                pl.BlockSpec((None, d, tf), w_map),
            ],
            out_specs=pl.BlockSpec((tm, tf), lambda t, j, te, nu: (t, j)),
            scratch_shapes=[pltpu.VMEM((tm, d), BF16)],
        ),
        out_shape=jax.ShapeDtypeStruct((s, ff), BF16),
        compiler_params=_params(("arbitrary", "arbitrary")),
        name="moe_experts_in",
    )(tile_expert, n_used, hs, w1, w3)


def _moe_out_kernel(te_ref, nu_ref, u_ref, w2_ref, gate_ref, o_ref):
    t = pl.program_id(0)

    @pl.when(t < nu_ref[0])
    def _():
        o_ref[...] = gate_ref[...] * jnp.dot(u_ref[...], w2_ref[...], preferred_element_type=F32)

    @pl.when(t >= nu_ref[0])
    def _():
        o_ref[...] = jnp.zeros_like(o_ref)


def _moe_out(u, w2, slot_gate, tile_expert, n_used, tm):
    s, ff = u.shape
    d = w2.shape[2]
    tn = _pick(d, (512, 256, 128))
    n_n = d // tn

    def w_map(t, j, te, nu):
        return (te[t], 0, jnp.where(t < nu[0], j, n_n - 1))

    return pl.pallas_call(
        _moe_out_kernel,
        grid_spec=pltpu.PrefetchScalarGridSpec(
            num_scalar_prefetch=2,
            grid=(s // tm, n_n),
            in_specs=[
                pl.BlockSpec((tm, ff), lambda t, j, te, nu: (t, 0)),
                pl.BlockSpec((None, ff, tn), w_map),
                pl.BlockSpec((tm, 1), lambda t, j, te, nu: (t, 0)),
            ],
            out_specs=pl.BlockSpec((tm, tn), lambda t, j, te, nu: (t, j)),
        ),
        out_shape=jax.ShapeDtypeStruct((s, d), F32),
        compiler_params=_params(("arbitrary", "arbitrary")),
        name="moe_experts_out",
    )(tile_expert, n_used, u, w2, slot_gate)


def _moe_combine_kernel(x_ref, ya_ref, yb_ref, gate_ref, o_ref, *, tm, geo):
    i = pl.program_id(0)
    for u in range(tm // SUB):
        row = _mod_row(i * (tm // SUB) + u, geo)
        rs = slice(u * SUB, (u + 1) * SUB)
        o_ref[rs, :] = x_ref[rs, :] + gate_ref[pl.ds(row, 1), :] * (ya_ref[rs, :] + yb_ref[rs, :])


def _moe_combine(x, ya, yb, mod, geo):
    m, d = x.shape
    tm = _pick(m, (512, 256))
    kern = functools.partial(_moe_combine_kernel, tm=tm, geo=geo)
    blk = pl.BlockSpec((tm, d), lambda i: (i, 0))
    return pl.pallas_call(
        kern,
        grid=(m // tm,),
        in_specs=[blk, blk, blk, pl.BlockSpec((mod.shape[0], d), lambda i: (0, 5))],
        out_specs=blk,
        out_shape=jax.ShapeDtypeStruct((m, d), F32),
        compiler_params=_params(("parallel",)),
        name="moe_combine",
    )(x, ya, yb, mod)


def _moe_plan(route, n_exp, tm, n_tiles):
    m = route.shape[0]
    e = route[:, 0:TOP_K].astype(jnp.int32).reshape(-1)
    gate = route[:, TOP_K:2 * TOP_K].reshape(-1)
    onehot = (e[:, None] == jnp.arange(n_exp, dtype=jnp.int32)[None, :]).astype(jnp.int32)
    csum = jnp.cumsum(onehot, axis=0)
    rank = jnp.sum(csum * onehot, axis=1) - 1
    counts = csum[-1]
    tiles_per = (counts + tm - 1) // tm
    tile_end = jnp.cumsum(tiles_per)
    tile_start = tile_end - tiles_per
    pos = tile_start[e] * tm + rank
    n_used = tile_end[-1]
    n_slots = n_tiles * tm
    token = jnp.arange(TOP_K * m, dtype=jnp.int32) // TOP_K
    slot_token = jnp.zeros((n_slots,), jnp.int32).at[pos].set(token)
    slot_gate = jnp.zeros((n_slots,), F32).at[pos].set(gate)
    tile_ids = jnp.arange(n_tiles, dtype=jnp.int32)
    tile_expert = jnp.sum((tile_ids[:, None] >= tile_end[None, :]).astype(jnp.int32), axis=1)
    last_expert = jnp.sum((n_used - 1 >= tile_end).astype(jnp.int32))
    tile_expert = jnp.where(tile_ids < n_used, tile_expert, last_expert).astype(jnp.int32)
    return pos.astype(jnp.int32), slot_token, slot_gate, tile_expert, n_used.reshape(1).astype(jnp.int32)


def _moe_layer(x, g, mod, rw_pad, w1, w3, w2, geo):
    m, d = x.shape
    n_exp = w1.shape[0]
    tm = 512
    n_tiles = -(-(TOP_K * m) // tm) + n_exp
    n_tiles = -(-(n_tiles * tm) // GATHER_ROWS) * GATHER_ROWS // tm
    h, route = _router(x, g, mod, rw_pad, n_exp, geo)
    pos, slot_token, slot_gate, tile_expert, n_used = _moe_plan(route, n_exp, tm, n_tiles)
    (hs,) = _gather_rows(h, [slot_token])
    u = _moe_in(hs, w1, w3, tile_expert, n_used, tm)
    y = _moe_out(u, w2, slot_gate.reshape(-1, 1), tile_expert, n_used, tm)
    mp = -(-m // GATHER_ROWS) * GATHER_ROWS
    pos2 = jnp.pad(pos.reshape(m, TOP_K), ((0, mp - m), (0, 0)))
    ya, yb = _gather_rows(y, [pos2[:, 0], pos2[:, 1]])
    return _moe_combine(x, ya[:m], yb[:m], mod, geo)


def _final_norm_kernel(x_ref, g_ref, o_ref):
    x = x_ref[...]
    ms = jnp.mean(x * x, axis=-1, keepdims=True)
    o_ref[...] = (x * lax.rsqrt(ms + EPS)) * g_ref[...]


def _final_norm(x3, g, ctx_len):
    b, t, d = x3.shape
    seq = t - ctx_len
    off = ctx_len // SUB
    return pl.pallas_call(
        _final_norm_kernel,
        grid=(b, seq // SUB),
        in_specs=[pl.BlockSpec((None, SUB, d), lambda bi, i: (bi, off + i, 0)),
                  pl.BlockSpec((1, d), lambda bi, i: (0, 0))],
        out_specs=pl.BlockSpec((None, SUB, d), lambda bi, i: (bi, i, 0)),
        out_shape=jax.ShapeDtypeStruct((b, seq, d), F32),
        compiler_params=_params(("parallel", "parallel")),
        name="final_norm",
    )(x3, g.reshape(1, d))


def _rope_tables(batch, seq, ctx_len):
    rows = seq // GRID_W
    row_pos = jnp.repeat(jnp.arange(rows, dtype=F32), GRID_W)
    col_pos = jnp.tile(jnp.arange(GRID_W, dtype=F32), rows)
    inv_freq = ROPE_THETA ** (-jnp.arange(ROPE_PAIRS_PER_AXIS, dtype=F32) / ROPE_PAIRS_PER_AXIS)
    ang = jnp.concatenate([row_pos[:, None] * inv_freq, col_pos[:, None] * inv_freq], axis=-1)
    cos = jnp.cos(ang)
    sin = jnp.sin(ang)
    zero = jnp.zeros_like(sin)
    cos128 = jnp.tile(cos, (1, 4))
    sa128 = jnp.tile(jnp.concatenate([-sin, zero], axis=-1), (1, 2))
    sb128 = jnp.tile(jnp.concatenate([zero, sin], axis=-1), (1, 2))

    def stream(tab, ctx_val):
        ctx_rows = jnp.full((ctx_len, LANES), ctx_val, F32)
        return jnp.tile(jnp.concatenate([ctx_rows, tab], axis=0), (batch, 1))

    return stream(cos128, 1.0), stream(sa128, 0.0), stream(sb128, 0.0)


def _pad_w_in(w_in):
    depth, d, _ = w_in.shape
    main = w_in[:, :, :COL_DT]
    dt = w_in[:, :, COL_DT:COL_DT + 2 * SSM_HEADS].reshape(depth, d, 2 * SSM_GROUPS, SSM_HEADS_PER_GROUP)
    dt = jnp.pad(dt, ((0, 0), (0, 0), (0, 0), (0, LANES - SSM_HEADS_PER_GROUP)))
    return jnp.concatenate([main, dt.reshape(depth, d, 2 * SSM_GROUPS * LANES)], axis=-1).astype(BF16)


def _lane_rows(v):
    depth = v.shape[0]
    v = v.reshape(depth, 2 * SSM_GROUPS, 1, SSM_HEADS_PER_GROUP)
    return jnp.pad(v, ((0, 0), (0, 0), (0, 0), (0, LANES - SSM_HEADS_PER_GROUP)))


def kernel(x, c, ctx, c_ctx, w_mod, b_mod, norm_mix_g, w_in, lambda_q1, lambda_k1, lambda_q2, lambda_k2, subln_g, conv_w, conv_b, a_log, dt_bias, d_skip, ssm_norm_g, w_out, norm_ffn_g, ffn_w1, ffn_w3, ffn_w2, router_w, moe_w1, moe_w3, moe_w2, final_g):
    batch, seq, d = x.shape
    ctx_len = ctx.shape[1]
    depth = w_mod.shape[0]
    t = ctx_len + seq
    m = batch * t
    assert ctx_len % SUB == 0 and seq % SUB == 0 and seq % GRID_W == 0 and batch < 8
    geo = (t // SUB, ctx_len // SUB, batch)

    w_in_p = _pad_w_in(w_in)
    w_out_b = w_out.astype(BF16)
    ffn_w1_b, ffn_w3_b, ffn_w2_b = ffn_w1.astype(BF16), ffn_w3.astype(BF16), ffn_w2.astype(BF16)
    moe_w1_b, moe_w3_b, moe_w2_b = moe_w1.astype(BF16), moe_w3.astype(BF16), moe_w2.astype(BF16)
    n_exp = router_w.shape[-1]
    rw_pad = jnp.pad(router_w, ((0, 0), (0, 0), (0, LANES - n_exp))).astype(BF16)

    c_all = jnp.zeros((8, d), F32).at[:batch].set(c).at[batch].set(c_ctx)
    mod_all = _mod_table(c_all, w_mod, b_mod)
    cos, sa, sb = _rope_tables(batch, seq, ctx_len)
    bias_rows = _lane_rows(dt_bias)
    alog_rows = _lane_rows(a_log)
    dskip_rows = jnp.repeat(d_skip, SSM_HEAD_DIM, axis=-1).reshape(depth, SSM_GROUPS, 1, GROUP_WIDTH)

    xs = jnp.concatenate([ctx, x], axis=1).reshape(m, d)
    for li in range(depth):
        lambda_init = 0.8 - 0.6 * math.exp(-0.3 * li)
        mod = mod_all[li]
        lam = (jnp.exp(jnp.sum(lambda_q1[li] * lambda_k1[li])) - jnp.exp(jnp.sum(lambda_q2[li] * lambda_k2[li]))
               + lambda_init).reshape(1).astype(F32)

        p = _in_proj(xs, norm_mix_g[li], mod, w_in_p[li], cos, sa, sb, geo)
        p3 = p.reshape(batch, t, IN_COLS_PAD)
        att = _attention(p3, lam, subln_g[li], lambda_init, ctx_len)
        u3 = _conv_silu(p3, conv_w[li], conv_b[li], ctx_len)
        yf, yb = _ssd_scan(u3, p3, bias_rows[li], alog_rows[li], dskip_rows[li], ctx_len)
        ssm = _ssd_finish(yf.reshape(m, SSM_WIDTH), yb.reshape(m, SSM_WIDTH), p, ssm_norm_g[li])
        xs = _mm_resid([att.reshape(m, ATT_WIDTH), ssm], [w_out_b[li, :ATT_WIDTH], w_out_b[li, ATT_WIDTH:]],
                       xs, mod, 2, geo, tn=512)

        j = li // 2
        if li % 2 == 0:
            uu = _ffn_in(xs, norm_ffn_g[li], mod, ffn_w1_b[j], ffn_w3_b[j], geo)
            xs = _mm_resid([uu], [ffn_w2_b[j]], xs, mod, 5, geo, tn=256)
        else:
            xs = _moe_layer(xs, norm_ffn_g[li], mod, rw_pad[j], moe_w1_b[j], moe_w3_b[j], moe_w2_b[j], geo)
    return _final_norm(xs.reshape(batch, t, d), final_g, ctx_len)
```

```python
import functools
import math

import jax
import jax.numpy as jnp
from jax import lax
from jax.experimental import pallas as pl
from jax.experimental.pallas import tpu as pltpu

F32 = jnp.float32
BF16 = jnp.bfloat16

GRID_W = 64
EPS = 1e-6
N_MOD = 6
ATT_HEADS = 8
ATT_QK_DIM = 64
ATT_V_DIM = 128
ATT_WIDTH = ATT_HEADS * ATT_V_DIM
ROPE_THETA = 10000.0
ROPE_PAIRS_PER_AXIS = ATT_QK_DIM // 4
SSM_HEAD_DIM = 64
SSM_GROUPS = 2
SSM_HEADS_PER_GROUP = 8
SSM_HEADS = SSM_GROUPS * SSM_HEADS_PER_GROUP
SSM_WIDTH = SSM_HEADS * SSM_HEAD_DIM
SSM_STATE = 128
SSM_CONV = 5
SSM_CHUNK = 128
GROUP_WIDTH = SSM_WIDTH // SSM_GROUPS
XBC_COLS = SSM_WIDTH + 2 * SSM_GROUPS * SSM_STATE
TOP_K = 2

LANES = 128
SUB = 256
VMEM_LIMIT = 52 * 1024 * 1024

COL_Q = 0
COL_K = COL_Q + ATT_WIDTH
COL_V = COL_K + ATT_WIDTH
COL_Z = COL_V + ATT_WIDTH
COL_XBC = COL_Z + SSM_WIDTH
COL_DT = COL_XBC + XBC_COLS
IN_COLS_PAD = COL_DT + 2 * SSM_GROUPS * LANES


def _pick(n, cands):
    for c in cands:
        if n % c == 0:
            return c
    raise ValueError(f"no tile in {cands} divides {n}")


def _params(sem, vmem=VMEM_LIMIT, **kw):
    return pltpu.CompilerParams(dimension_semantics=sem, vmem_limit_bytes=vmem, **kw)


def _mod_row(sub_idx, geo):
    subs_per_batch, ctx_subs, n_batch = geo
    b = sub_idx // subs_per_batch
    w = sub_idx % subs_per_batch
    return jnp.where(w < ctx_subs, n_batch, b)


def _normmod(x, g, shift, scale):
    ms = jnp.mean(x * x, axis=-1, keepdims=True)
    hn = (x * lax.rsqrt(ms + EPS)) * g
    return hn * (1.0 + scale) + shift


def _silu(x):
    return x * (1.0 / (1.0 + jnp.exp(-x)))


def _mod_kernel(c_ref, w_ref, b_ref, o_ref):
    s = _silu(c_ref[...]).astype(BF16)
    o_ref[...] = jnp.dot(s, w_ref[...].astype(BF16), preferred_element_type=F32) + b_ref[...]


def _mod_table(c_all, w_mod, b_mod):
    depth, d, n = w_mod.shape
    tn = _pick(n, (1024, 512, 256, 128))
    return pl.pallas_call(
        _mod_kernel,
        grid=(depth, n // tn),
        in_specs=[
            pl.BlockSpec((8, d), lambda l, j: (0, 0)),
            pl.BlockSpec((None, d, tn), lambda l, j: (l, 0, j)),
            pl.BlockSpec((None, 1, tn), lambda l, j: (l, 0, j)),
        ],
        out_specs=pl.BlockSpec((None, 8, tn), lambda l, j: (l, 0, j)),
        out_shape=jax.ShapeDtypeStruct((depth, 8, n), F32),
        compiler_params=_params(("parallel", "parallel")),
        name="mod_table",
    )(c_all, w_mod, b_mod.reshape(depth, 1, n))


def _in_proj_kernel(x_ref, g_ref, mod_ref, w_ref, cos_ref, sa_ref, sb_ref, o_ref, h_ref, *, tm, tn, d, geo):
    i = pl.program_id(0)
    j = pl.program_id(1)

    @pl.when(j == 0)
    def _():
        for u in range(tm // SUB):
            row = _mod_row(i * (tm // SUB) + u, geo)
            rs = slice(u * SUB, (u + 1) * SUB)
            shift = mod_ref[pl.ds(row, 1), 0:d]
            scale = mod_ref[pl.ds(row, 1), d:2 * d]
            h_ref[rs, :] = _normmod(x_ref[rs, :], g_ref[...], shift, scale).astype(BF16)

    acc = jnp.dot(h_ref[...], w_ref[...], preferred_element_type=F32)

    def rope(scale):
        outs = []
        for hs in range(tn // LANES):
            a = acc[:, hs * LANES:(hs + 1) * LANES]
            r = (a * cos_ref[...] + pltpu.roll(a, LANES - 32, 1) * sa_ref[...]
                 + pltpu.roll(a, 32, 1) * sb_ref[...])
            outs.append(r * scale)
        return jnp.concatenate(outs, axis=1)

    @pl.when(j == COL_Q // tn)
    def _():
        o_ref[...] = rope(ATT_QK_DIM ** -0.5).astype(BF16)

    @pl.when(j == COL_K // tn)
    def _():
        o_ref[...] = rope(1.0).astype(BF16)

    @pl.when(j >= COL_V // tn)
    def _():
        o_ref[...] = acc.astype(BF16)


def _in_proj(x, g, mod, w, cos, sa, sb, geo):
    m, d = x.shape
    n = w.shape[1]
    tm = _pick(m, (1024, 512, 256))
    tn = ATT_WIDTH
    kern = functools.partial(_in_proj_kernel, tm=tm, tn=tn, d=d, geo=geo)
    return pl.pallas_call(
        kern,
        grid=(m // tm, n // tn),
        in_specs=[
            pl.BlockSpec((tm, d), lambda i, j: (i, 0)),
            pl.BlockSpec((1, d), lambda i, j: (0, 0)),
            pl.BlockSpec(mod.shape, lambda i, j: (0, 0)),
            pl.BlockSpec((d, tn), lambda i, j: (0, j)),
            pl.BlockSpec((tm, LANES), lambda i, j: (i, 0)),
            pl.BlockSpec((tm, LANES), lambda i, j: (i, 0)),
            pl.BlockSpec((tm, LANES), lambda i, j: (i, 0)),
        ],
        out_specs=pl.BlockSpec((tm, tn), lambda i, j: (i, j)),
        out_shape=jax.ShapeDtypeStruct((m, n), BF16),
        scratch_shapes=[pltpu.VMEM((tm, d), BF16)],
        compiler_params=_params(("parallel", "arbitrary")),
        name="in_proj",
    )(x, g.reshape(1, d), mod, w, cos, sa, sb)


def _attn_kernel(lam_ref, q_ref, k_ref, v_ref, g_ref, o_ref, *, tq, tk, ctx_tiles, ctx_chunks, all_chunks,
                 out_scale):
    qi = pl.program_id(2)
    q = q_ref[...]
    lane = lax.broadcasted_iota(jnp.int32, (tq, LANES), 1)
    zero = jnp.zeros_like(q)
    q2 = jnp.concatenate([jnp.where(lane < ATT_QK_DIM, q, zero), jnp.where(lane >= ATT_QK_DIM, q, zero)], axis=0)
    n_chunks = jnp.where(qi < ctx_tiles, ctx_chunks, all_chunks)

    def body(c, carry):
        m, l, acc = carry
        off = pl.multiple_of(c * tk, tk)
        kc = k_ref[pl.ds(off, tk), :]
        vc = v_ref[pl.ds(off, tk), :]
        s = lax.dot_general(q2, kc, (((1,), (1,)), ((), ())), preferred_element_type=F32)
        m_new = jnp.maximum(m, jnp.max(s, axis=-1, keepdims=True))
        p = jnp.exp(s - m_new)
        alpha = jnp.exp(m - m_new)
        l = alpha * l + jnp.sum(p, axis=-1, keepdims=True)
        acc = alpha * acc + jnp.dot(p.astype(BF16), vc, preferred_element_type=F32)
        return m_new, l, acc

    m0 = jnp.full((2 * tq, 1), -1e30, F32)
    l0 = jnp.zeros((2 * tq, 1), F32)
    a0 = jnp.zeros((2 * tq, LANES), F32)
    _, l, acc = lax.fori_loop(0, n_chunks, body, (m0, l0, a0))
    o = acc * (1.0 / l)
    o = o[:tq] - lam_ref[0] * o[tq:]
    ms = jnp.mean(o * o, axis=-1, keepdims=True)
    o = (o * lax.rsqrt(ms + EPS)) * g_ref[...] * out_scale
    o_ref[...] = o.astype(BF16)


def _attention(p3, lam, subln_g, lambda_init, ctx_len):
    b, t, _ = p3.shape
    tq = SUB
    tk = SUB
    kern = functools.partial(_attn_kernel, tq=tq, tk=tk, ctx_tiles=ctx_len // tq, ctx_chunks=ctx_len // tk,
                             all_chunks=t // tk, out_scale=1.0 - lambda_init)
    return pl.pallas_call(
        kern,
        grid=(b, ATT_HEADS, t // tq),
        in_specs=[
            pl.BlockSpec(memory_space=pltpu.SMEM),
            pl.BlockSpec((None, tq, LANES), lambda bi, h, qi: (bi, qi, COL_Q // LANES + h)),
            pl.BlockSpec((None, t, LANES), lambda bi, h, qi: (bi, 0, COL_K // LANES + h)),
            pl.BlockSpec((None, t, LANES), lambda bi, h, qi: (bi, 0, COL_V // LANES + h)),
            pl.BlockSpec((1, LANES), lambda bi, h, qi: (0, 0)),
        ],
        out_specs=pl.BlockSpec((None, tq, LANES), lambda bi, h, qi: (bi, qi, h)),
        out_shape=jax.ShapeDtypeStruct((b, t, ATT_WIDTH), BF16),
        compiler_params=_params(("parallel", "parallel", "arbitrary")),
        name="diff_attention",
    )(lam, p3, p3, p3, subln_g.reshape(1, LANES))


CONV_HALO = 16


def _conv_kernel(prev_ref, cur_ref, next_ref, w_ref, b_ref, o_ref, *, rb, ctx_blocks, n_blocks):
    i = pl.program_id(1)
    prev_ok = jnp.logical_and(i != 0, i != ctx_blocks)
    next_ok = jnp.logical_and(i != ctx_blocks - 1, i != n_blocks - 1)
    prev = jnp.where(prev_ok, prev_ref[...].astype(F32), 0.0)
    nxt = jnp.where(next_ok, next_ref[...].astype(F32), 0.0)
    xc = jnp.concatenate([prev, cur_ref[...].astype(F32), nxt], axis=0)
    pad = SSM_CONV // 2
    acc = b_ref[...] + w_ref[0:1, :] * xc[CONV_HALO - pad:CONV_HALO - pad + rb, :]
    for k in range(1, SSM_CONV):
        s = CONV_HALO - pad + k
        acc = acc + w_ref[k:k + 1, :] * xc[s:s + rb, :]
    o_ref[...] = _silu(acc).astype(BF16)


def _conv_silu(p3, conv_w, conv_b, ctx_len):
    b, t, _ = p3.shape
    rb = SUB
    tc = 512
    hb = rb // CONV_HALO
    n_blocks = t // rb
    c0 = COL_XBC // tc
    kern = functools.partial(_conv_kernel, rb=rb, ctx_blocks=ctx_len // rb, n_blocks=n_blocks)
    last_halo = t // CONV_HALO - 1
    return pl.pallas_call(
        kern,
        grid=(b, n_blocks, XBC_COLS // tc),
        in_specs=[
            pl.BlockSpec((None, CONV_HALO, tc), lambda bi, i, j: (bi, jnp.maximum(i * hb - 1, 0), c0 + j)),
            pl.BlockSpec((None, rb, tc), lambda bi, i, j: (bi, i, c0 + j)),
            pl.BlockSpec((None, CONV_HALO, tc), lambda bi, i, j: (bi, jnp.minimum((i + 1) * hb, last_halo), c0 + j)),
            pl.BlockSpec((SSM_CONV, tc), lambda bi, i, j: (0, j)),
            pl.BlockSpec((1, tc), lambda bi, i, j: (0, j)),
        ],
        out_specs=pl.BlockSpec((None, rb, tc), lambda bi, i, j: (bi, i, j)),
        out_shape=jax.ShapeDtypeStruct((b, t, XBC_COLS), BF16),
        compiler_params=_params(("parallel", "parallel", "parallel")),
        name="conv_silu",
    )(p3, p3, p3, conv_w, conv_b.reshape(1, XBC_COLS))


def _split_bf16(a, pieces):
    parts = []
    r = a
    for _ in range(pieces):
        p = r.astype(BF16)
        parts.append(p)
        r = r - p.astype(F32)
    return parts


def _dot_sel(a, sel, pieces):
    out = None
    for p in _split_bf16(a, pieces):
        t = jnp.dot(p, sel, preferred_element_type=F32)
        out = t if out is None else out + t
    return out


def _sel_dot(sel, a, pieces):
    out = None
    for p in _split_bf16(a, pieces):
        t = jnp.dot(sel, p, preferred_element_type=F32)
        out = t if out is None else out + t
    return out


def _ssd_chunk(xs, bm, cm, dtb, bias, alog, h_ref, tri, ex64, ex128, reverse):
    q = xs.shape[0]
    hp = SSM_HEADS_PER_GROUP
    dt = jax.nn.softplus(dtb.astype(F32) + bias)
    a = dt * (-jnp.exp(alog))
    acs = _sel_dot(tri, a, 3)
    acs_t = acs.T
    end = acs[0:1, :] if reverse else acs[q - 1:q, :]
    dt_e = _dot_sel(dt, ex64, 2)
    eacs_e = _dot_sel(jnp.exp(acs), ex64, 2)
    dte_e = _dot_sel(jnp.exp(end - acs), ex64, 2)
    colb = _dot_sel(acs, ex128, 3)
    xsf = xs.astype(F32)
    xdt = xsf * dt_e
    cb = lax.dot_general(cm, bm, (((1,), (1,)), ((), ())), preferred_element_type=F32)
    qi = lax.broadcasted_iota(jnp.int32, (q, q), 0)
    si = lax.broadcasted_iota(jnp.int32, (q, q), 1)
    mask = (si >= qi) if reverse else (si <= qi)
    lane = lax.broadcasted_iota(jnp.int32, (q, LANES), 1)
    ys = []
    for j in range(hp // 2):
        xp = xdt[:, j * LANES:(j + 1) * LANES]
        halves = (jnp.where(lane < SSM_HEAD_DIM, xp, 0.0).astype(BF16),
                  jnp.where(lane >= SSM_HEAD_DIM, xp, 0.0).astype(BF16))
        acc = None
        for k in range(2):
            r = 2 * j + k
            seg = colb[:, r * LANES:(r + 1) * LANES] - acs_t[r:r + 1, :]
            dec = jnp.exp(jnp.where(mask, seg, -1e30))
            t = jnp.dot((cb * dec).astype(BF16), halves[k], preferred_element_type=F32)
            acc = t if acc is None else acc + t
        ys.append(acc)
    y_diag = jnp.concatenate(ys, axis=1)
    h = h_ref[...]
    y_off = jnp.dot(cm, h.astype(BF16), preferred_element_type=F32) * eacs_e
    cdec = eacs_e[0:1, :] if reverse else eacs_e[q - 1:q, :]
    xd = (xdt * dte_e).astype(BF16)
    h_ref[...] = h * cdec + lax.dot_general(bm, xd, (((0,), (0,)), ((), ())), preferred_element_type=F32)
    return y_diag + y_off, xsf


def _ssd_kernel(xsf_ref, bmf_ref, cmf_ref, dtf_ref, xsb_ref, bmb_ref, cmb_ref, dtb_ref,
                biasf_ref, biasb_ref, alogf_ref, alogb_ref, dskip_ref, tril_ref, triu_ref, ex64_ref, ex128_ref,
                yf_ref, yb_ref, hf_ref, hb_ref):
    @pl.when(pl.program_id(2) == 0)
    def _():
        hf_ref[...] = jnp.zeros_like(hf_ref)
        hb_ref[...] = jnp.zeros_like(hb_ref)

    yf, xsf = _ssd_chunk(xsf_ref[...], bmf_ref[...], cmf_ref[...], dtf_ref[...], biasf_ref[...], alogf_ref[...],
                         hf_ref, tril_ref[...], ex64_ref[...], ex128_ref[...], False)
    yf_ref[...] = yf + dskip_ref[...] * xsf
    yb, _ = _ssd_chunk(xsb_ref[...], bmb_ref[...], cmb_ref[...], dtb_ref[...], biasb_ref[...], alogb_ref[...],
                       hb_ref, triu_ref[...], ex64_ref[...], ex128_ref[...], True)
    yb_ref[...] = yb


def _ssd_scan(u3, p3, bias4, alog4, dskip, ctx_len):
    b, t, _ = u3.shape
    q = SSM_CHUNK
    ns = t // q
    ncc = ctx_len // q
    g_n = SSM_GROUPS

    def fwd(s):
        return s

    def bwd(s):
        return jnp.where(s < ncc, ncc - 1 - s, ns - 1 + ncc - s)

    xs_blk = GROUP_WIDTH // LANES
    b_col = SSM_WIDTH // LANES
    c_col = b_col + g_n
    dt_col = COL_DT // LANES
    del xs_blk

    def specs(cidx, direction):
        return [
            pl.BlockSpec((None, q, GROUP_WIDTH), lambda bi, g, s: (bi, cidx(s), g)),
            pl.BlockSpec((None, q, LANES), lambda bi, g, s: (bi, cidx(s), b_col + g)),
            pl.BlockSpec((None, q, LANES), lambda bi, g, s: (bi, cidx(s), c_col + g)),
            pl.BlockSpec((None, q, LANES), lambda bi, g, s: (bi, cidx(s), dt_col + direction * g_n + g)),
        ]

    row = lambda direction: pl.BlockSpec((None, 1, LANES), lambda bi, g, s: (direction * g_n + g, 0, 0))
    full = lambda arr: pl.BlockSpec(arr.shape, lambda bi, g, s: (0,) * arr.ndim)

    ri = lax.broadcasted_iota(jnp.int32, (q, q), 0)
    ci = lax.broadcasted_iota(jnp.int32, (q, q), 1)
    tril = (ci <= ri).astype(BF16)
    triu = (ci >= ri).astype(BF16)
    r64 = lax.broadcasted_iota(jnp.int32, (LANES, GROUP_WIDTH), 0)
    c64 = lax.broadcasted_iota(jnp.int32, (LANES, GROUP_WIDTH), 1)
    ex64 = (c64 // SSM_HEAD_DIM == r64).astype(BF16)
    r128 = lax.broadcasted_iota(jnp.int32, (LANES, SSM_HEADS_PER_GROUP * LANES), 0)
    c128 = lax.broadcasted_iota(jnp.int32, (LANES, SSM_HEADS_PER_GROUP * LANES), 1)
    ex128 = (c128 // LANES == r128).astype(BF16)

    in_specs = (specs(fwd, 0) + specs(bwd, 1) + [row(0), row(1), row(0), row(1),
                pl.BlockSpec((None, 1, GROUP_WIDTH), lambda bi, g, s: (g, 0, 0)),
                full(tril), full(triu), full(ex64), full(ex128)])
    out_spec = lambda cidx: pl.BlockSpec((None, q, GROUP_WIDTH), lambda bi, g, s: (bi, cidx(s), g))
    return pl.pallas_call(
        _ssd_kernel,
        grid=(b, g_n, ns),
        in_specs=in_specs,
        out_specs=[out_spec(fwd), out_spec(bwd)],
        out_shape=[jax.ShapeDtypeStruct((b, t, SSM_WIDTH), F32)] * 2,
        scratch_shapes=[pltpu.VMEM((SSM_STATE, GROUP_WIDTH), F32)] * 2,
        compiler_params=_params(("parallel", "parallel", "arbitrary")),
        name="ssd_scan",
    )(u3, u3, u3, p3, u3, u3, u3, p3, bias4, bias4, alog4, alog4, dskip, tril, triu, ex64, ex128)


def _ssd_finish_kernel(yf_ref, yb_ref, z_ref, g_ref, o_ref):
    y = (yf_ref[...] + yb_ref[...]) * _silu(z_ref[...].astype(F32))
    outs = []
    for g in range(SSM_GROUPS):
        yg = y[:, g * GROUP_WIDTH:(g + 1) * GROUP_WIDTH]
        ms = jnp.mean(yg * yg, axis=-1, keepdims=True)
        outs.append((yg * lax.rsqrt(ms + EPS)) * g_ref[:, g * GROUP_WIDTH:(g + 1) * GROUP_WIDTH])
    o_ref[...] = jnp.concatenate(outs, axis=1).astype(BF16)


def _ssd_finish(yf, yb, p, norm_g):
    m = yf.shape[0]
    tm = _pick(m, (512, 256))
    return pl.pallas_call(
        _ssd_finish_kernel,
        grid=(m // tm,),
        in_specs=[
            pl.BlockSpec((tm, SSM_WIDTH), lambda i: (i, 0)),
            pl.BlockSpec((tm, SSM_WIDTH), lambda i: (i, 0)),
            pl.BlockSpec((tm, SSM_WIDTH), lambda i: (i, COL_Z // SSM_WIDTH)),
            pl.BlockSpec((1, SSM_WIDTH), lambda i: (0, 0)),
        ],
        out_specs=pl.BlockSpec((tm, SSM_WIDTH), lambda i: (i, 0)),
        out_shape=jax.ShapeDtypeStruct((m, SSM_WIDTH), BF16),
        compiler_params=_params(("parallel",)),
        name="ssd_finish",
    )(yf, yb, p, norm_g.reshape(1, SSM_WIDTH))


def _mm_resid_kernel(*refs, n_a, tm, geo):
    a_refs = refs[:n_a]
    w_refs = refs[n_a:2 * n_a]
    x_ref, gate_ref, o_ref = refs[2 * n_a:]
    i = pl.program_id(0)
    acc = None
    for a_ref, w_ref in zip(a_refs, w_refs):
        t = jnp.dot(a_ref[...], w_ref[...], preferred_element_type=F32)
        acc = t if acc is None else acc + t
    for u in range(tm // SUB):
        row = _mod_row(i * (tm // SUB) + u, geo)
        rs = slice(u * SUB, (u + 1) * SUB)
        o_ref[rs, :] = x_ref[rs, :] + gate_ref[pl.ds(row, 1), :] * acc[rs, :]


def _mm_resid(a_list, w_list, x, mod, gate_idx, geo, tn):
    m, d = x.shape
    tm = _pick(m, (1024, 512, 256))
    n_a = len(a_list)
    kern = functools.partial(_mm_resid_kernel, n_a=n_a, tm=tm, geo=geo)
    gate_blk = gate_idx * d // tn
    in_specs = ([pl.BlockSpec((tm, a.shape[1]), lambda i, j: (i, 0)) for a in a_list]
                + [pl.BlockSpec((w.shape[0], tn), lambda i, j: (0, j)) for w in w_list]
                + [pl.BlockSpec((tm, tn), lambda i, j: (i, j)),
                   pl.BlockSpec((mod.shape[0], tn), lambda i, j: (0, gate_blk + j))])
    return pl.pallas_call(
        kern,
        grid=(m // tm, d // tn),
        in_specs=in_specs,
        out_specs=pl.BlockSpec((tm, tn), lambda i, j: (i, j)),
        out_shape=jax.ShapeDtypeStruct((m, d), F32),
        compiler_params=_params(("parallel", "parallel")),
        name="matmul_gated_residual",
    )(*a_list, *w_list, x, mod)


def _ffn_in_kernel(x_ref, g_ref, mod_ref, w1_ref, w3_ref, o_ref, h_ref, *, tm, d, geo):
    i = pl.program_id(0)

    @pl.when(pl.program_id(1) == 0)
    def _():
        for u in range(tm // SUB):
            row = _mod_row(i * (tm // SUB) + u, geo)
            rs = slice(u * SUB, (u + 1) * SUB)
            shift = mod_ref[pl.ds(row, 1), 3 * d:4 * d]
            scale = mod_ref[pl.ds(row, 1), 4 * d:5 * d]
            h_ref[rs, :] = _normmod(x_ref[rs, :], g_ref[...], shift, scale).astype(BF16)

    h = h_ref[...]
    a = jnp.dot(h, w1_ref[...], preferred_element_type=F32)
    b = jnp.dot(h, w3_ref[...], preferred_element_type=F32)
    o_ref[...] = (_silu(a) * b).astype(BF16)


def _ffn_in(x, g, mod, w1, w3, geo):
    m, d = x.shape
    ff = w1.shape[1]
    tm = _pick(m, (1024, 512, 256))
    tf = _pick(ff, (512, 256, 128))
    kern = functools.partial(_ffn_in_kernel, tm=tm, d=d, geo=geo)
    return pl.pallas_call(
        kern,
        grid=(m // tm, ff // tf),
        in_specs=[
            pl.BlockSpec((tm, d), lambda i, j: (i, 0)),
            pl.BlockSpec((1, d), lambda i, j: (0, 0)),
            pl.BlockSpec(mod.shape, lambda i, j: (0, 0)),
            pl.BlockSpec((d, tf), lambda i, j: (0, j)),
            pl.BlockSpec((d, tf), lambda i, j: (0, j)),
        ],
        out_specs=pl.BlockSpec((tm, tf), lambda i, j: (i, j)),
        out_shape=jax.ShapeDtypeStruct((m, ff), BF16),
        scratch_shapes=[pltpu.VMEM((tm, d), BF16)],
        compiler_params=_params(("parallel", "arbitrary")),
        name="ffn_in",
    )(x, g.reshape(1, d), mod, w1, w3)


def _store_row_tiles(dst_ref, src_ref, rt):
    tm = src_ref.shape[0]
    for a in range(rt):
        dst_ref[pl.ds(a, tm, stride=rt), :] = src_ref[:, a * LANES:(a + 1) * LANES]


def _load_row_tiles(src_ref, row0, tm, rt):
    return jnp.concatenate([src_ref[pl.ds(row0 * rt + a, tm, stride=rt), :] for a in range(rt)], axis=1)


def _router_kernel(x_ref, g_ref, mod_ref, rw_ref, h_ref, r_ref, hs_ref, *, tm, d, n_exp, geo):
    i = pl.program_id(0)
    for u in range(tm // SUB):
        row = _mod_row(i * (tm // SUB) + u, geo)
        rs = slice(u * SUB, (u + 1) * SUB)
        shift = mod_ref[pl.ds(row, 1), 3 * d:4 * d]
        scale = mod_ref[pl.ds(row, 1), 4 * d:5 * d]
        hs_ref[rs, :] = _normmod(x_ref[rs, :], g_ref[...], shift, scale)
    _store_row_tiles(h_ref, hs_ref, d // LANES)
    logits = jnp.dot(hs_ref[...].astype(BF16), rw_ref[...], preferred_element_type=F32)
    lane = lax.broadcasted_iota(jnp.int32, logits.shape, 1)
    neg = jnp.float32(-1e30)
    logits = jnp.where(lane < n_exp, logits, neg)
    m1 = jnp.max(logits, axis=-1, keepdims=True)
    i1 = jnp.min(jnp.where(logits == m1, lane, LANES), axis=-1, keepdims=True)
    rest = jnp.where(lane == i1, neg, logits)
    m2 = jnp.max(rest, axis=-1, keepdims=True)
    i2 = jnp.min(jnp.where(rest == m2, lane, LANES), axis=-1, keepdims=True)
    t = jnp.exp(m2 - m1)
    g1 = 1.0 / (1.0 + t)
    g2 = t * g1
    out = jnp.where(lane == 0, i1.astype(F32),
                    jnp.where(lane == 1, i2.astype(F32),
                              jnp.where(lane == 2, g1, jnp.where(lane == 3, g2, 0.0))))
    r_ref[...] = out


def _router(x, g, mod, rw_pad, n_exp, geo):
    m, d = x.shape
    rt = d // LANES
    tm = _pick(m, (512, 256))
    kern = functools.partial(_router_kernel, tm=tm, d=d, n_exp=n_exp, geo=geo)
    return pl.pallas_call(
        kern,
        grid=(m // tm,),
        in_specs=[
            pl.BlockSpec((tm, d), lambda i: (i, 0)),
            pl.BlockSpec((1, d), lambda i: (0, 0)),
            pl.BlockSpec(mod.shape, lambda i: (0, 0)),
            pl.BlockSpec((d, LANES), lambda i: (0, 0)),
        ],
        out_specs=[pl.BlockSpec((tm * rt, LANES), lambda i: (i, 0)), pl.BlockSpec((tm, LANES), lambda i: (i, 0))],
        out_shape=[jax.ShapeDtypeStruct((m * rt, LANES), F32), jax.ShapeDtypeStruct((m, LANES), F32)],
        scratch_shapes=[pltpu.VMEM((tm, d), F32)],
        compiler_params=_params(("parallel",)),
        name="moe_router",
    )(x, g.reshape(1, d), mod, rw_pad)


GATHER_ROWS = 1024


def _gather_kernel(*refs, n_out, rows, rt):
    idx_refs = refs[:n_out]
    src_ref = refs[n_out]
    dst_refs = refs[n_out + 1:2 * n_out + 1]
    sem = refs[2 * n_out + 1]
    base = pl.program_id(0) * rows

    def copy(k, r):
        src_row = pl.multiple_of(idx_refs[k][r] * rt, rt)
        dst_row = pl.multiple_of((base + r) * rt, rt)
        return pltpu.make_async_copy(src_ref.at[pl.ds(src_row, rt), :], dst_refs[k].at[pl.ds(dst_row, rt), :], sem)

    def start(r, c):
        for k in range(n_out):
            copy(k, r).start()
        return c

    def wait(r, c):
        for k in range(n_out):
            copy(k, r).wait()
        return c

    lax.fori_loop(0, rows, start, 0)
    lax.fori_loop(0, rows, wait, 0)


def _gather_rows(src, idx_list, rt):
    n = idx_list[0].shape[0]
    rows = GATHER_ROWS
    assert n % rows == 0
    n_out = len(idx_list)
    kern = functools.partial(_gather_kernel, n_out=n_out, rows=rows, rt=rt)
    outs = pl.pallas_call(
        kern,
        grid=(n // rows,),
        in_specs=([pl.BlockSpec((rows,), lambda i: (i,), memory_space=pltpu.SMEM) for _ in idx_list]
                  + [pl.BlockSpec(memory_space=pl.ANY)]),
        out_specs=[pl.BlockSpec(memory_space=pl.ANY) for _ in idx_list],
        out_shape=[jax.ShapeDtypeStruct((n * rt, LANES), src.dtype) for _ in idx_list],
        scratch_shapes=[pltpu.SemaphoreType.DMA(())],
        compiler_params=_params(("arbitrary",), has_side_effects=True),
        name="gather_rows",
    )(*idx_list, src)
    return outs


def _moe_in_kernel(te_ref, nu_ref, h_ref, w1_ref, w3_ref, o_ref, hb_ref):
    t = pl.program_id(0)
    j = pl.program_id(1)
    tm, d = hb_ref.shape

    @pl.when(t < nu_ref[0])
    def _():
        @pl.when(j == 0)
        def _():
            hb_ref[...] = _load_row_tiles(h_ref, 0, tm, d // LANES).astype(BF16)

        h = hb_ref[...]
        a = jnp.dot(h, w1_ref[...], preferred_element_type=F32)
        b = jnp.dot(h, w3_ref[...], preferred_element_type=F32)
        o_ref[...] = (_silu(a) * b).astype(BF16)

    @pl.when(t >= nu_ref[0])
    def _():
        o_ref[...] = jnp.zeros_like(o_ref)


def _moe_in(hs, w1, w3, tile_expert, n_used, tm):
    d = w1.shape[1]
    rt = d // LANES
    s = hs.shape[0] // rt
    ff = w1.shape[2]
    tf = _pick(ff, (512, 256, 128))
    n_ff = ff // tf

    def w_map(t, j, te, nu):
        return (te[t], 0, jnp.where(t < nu[0], j, n_ff - 1))

    return pl.pallas_call(
        _moe_in_kernel,
        grid_spec=pltpu.PrefetchScalarGridSpec(
            num_scalar_prefetch=2,
            grid=(s // tm, n_ff),
            in_specs=[
                pl.BlockSpec((tm * rt, LANES), lambda t, j, te, nu: (t, 0)),
                pl.BlockSpec((None, d, tf), w_map),
                pl.BlockSpec((None, d, tf), w_map),
            ],
            out_specs=pl.BlockSpec((tm, tf), lambda t, j, te, nu: (t, j)),
            scratch_shapes=[pltpu.VMEM((tm, d), BF16)],
        ),
        out_shape=jax.ShapeDtypeStruct((s, ff), BF16),
        compiler_params=_params(("arbitrary", "arbitrary")),
        name="moe_experts_in",
    )(tile_expert, n_used, hs, w1, w3)


def _moe_out_kernel(te_ref, nu_ref, u_ref, w2_ref, gate_ref, o_ref, acc_ref, *, n_k):
    t = pl.program_id(0)
    k = pl.program_id(1)
    valid = t < nu_ref[0]

    @pl.when(valid)
    def _():
        part = jnp.dot(u_ref[...], w2_ref[...], preferred_element_type=F32)

        @pl.when(k == 0)
        def _():
            acc_ref[...] = part

        @pl.when(k > 0)
        def _():
            acc_ref[...] += part

        @pl.when(k == n_k - 1)
        def _():
            acc_ref[...] = gate_ref[...] * acc_ref[...]
            _store_row_tiles(o_ref, acc_ref, acc_ref.shape[1] // LANES)

    @pl.when(jnp.logical_and(jnp.logical_not(valid), k == n_k - 1))
    def _():
        o_ref[...] = jnp.zeros_like(o_ref)


def _moe_out(u, w2, slot_gate, tile_expert, n_used, tm):
    s, ff = u.shape
    d = w2.shape[2]
    rt = d // LANES
    tk = _pick(ff, (1408, 1024, 512, 256, 128))
    n_k = ff // tk

    def k_idx(t, k, nu):
        return jnp.where(t < nu[0], k, n_k - 1)

    return pl.pallas_call(
        functools.partial(_moe_out_kernel, n_k=n_k),
        grid_spec=pltpu.PrefetchScalarGridSpec(
            num_scalar_prefetch=2,
            grid=(s // tm, n_k),
            in_specs=[
                pl.BlockSpec((tm, tk), lambda t, k, te, nu: (t, k_idx(t, k, nu))),
                pl.BlockSpec((None, tk, d), lambda t, k, te, nu: (te[t], k_idx(t, k, nu), 0)),
                pl.BlockSpec((tm, 1), lambda t, k, te, nu: (t, 0)),
            ],
            out_specs=pl.BlockSpec((tm * rt, LANES), lambda t, k, te, nu: (t, 0)),
            scratch_shapes=[pltpu.VMEM((tm, d), F32)],
        ),
        out_shape=jax.ShapeDtypeStruct((s * rt, LANES), F32),
        compiler_params=_params(("arbitrary", "arbitrary")),
        name="moe_experts_out",
    )(tile_expert, n_used, u, w2, slot_gate)


def _moe_combine_kernel(x_ref, ya_ref, yb_ref, gate_ref, o_ref, *, tm, geo):
    i = pl.program_id(0)
    rt = x_ref.shape[1] // LANES
    for u in range(tm // SUB):
        row = _mod_row(i * (tm // SUB) + u, geo)
        rs = slice(u * SUB, (u + 1) * SUB)
        y = _load_row_tiles(ya_ref, u * SUB, SUB, rt) + _load_row_tiles(yb_ref, u * SUB, SUB, rt)
        o_ref[rs, :] = x_ref[rs, :] + gate_ref[pl.ds(row, 1), :] * y


def _moe_combine(x, ya, yb, mod, geo):
    m, d = x.shape
    rt = d // LANES
    tm = _pick(m, (512, 256))
    kern = functools.partial(_moe_combine_kernel, tm=tm, geo=geo)
    blk = pl.BlockSpec((tm, d), lambda i: (i, 0))
    blk3 = pl.BlockSpec((tm * rt, LANES), lambda i: (i, 0))
    return pl.pallas_call(
        kern,
        grid=(m // tm,),
        in_specs=[blk, blk3, blk3, pl.BlockSpec((mod.shape[0], d), lambda i: (0, 5))],
        out_specs=blk,
        out_shape=jax.ShapeDtypeStruct((m, d), F32),
        compiler_params=_params(("parallel",)),
        name="moe_combine",
    )(x, ya, yb, mod)


def _moe_plan(route, n_exp, tm, n_tiles):
    m = route.shape[0]
    e = route[:, 0:TOP_K].astype(jnp.int32).reshape(-1)
    gate = route[:, TOP_K:2 * TOP_K].reshape(-1)
    onehot = (e[:, None] == jnp.arange(n_exp, dtype=jnp.int32)[None, :]).astype(jnp.int32)
    csum = jnp.cumsum(onehot, axis=0)
    rank = jnp.sum(csum * onehot, axis=1) - 1
    counts = csum[-1]
    tiles_per = (counts + tm - 1) // tm
    tile_end = jnp.cumsum(tiles_per)
    tile_start = tile_end - tiles_per
    pos = tile_start[e] * tm + rank
    n_used = tile_end[-1]
    n_slots = n_tiles * tm
    token = jnp.arange(TOP_K * m, dtype=jnp.int32) // TOP_K
    slot_token = jnp.zeros((n_slots,), jnp.int32).at[pos].set(token)
    slot_gate = jnp.zeros((n_slots,), F32).at[pos].set(gate)
    tile_ids = jnp.arange(n_tiles, dtype=jnp.int32)
    tile_expert = jnp.sum((tile_ids[:, None] >= tile_end[None, :]).astype(jnp.int32), axis=1)
    last_expert = jnp.sum((n_used - 1 >= tile_end).astype(jnp.int32))
    tile_expert = jnp.where(tile_ids < n_used, tile_expert, last_expert).astype(jnp.int32)
    return pos.astype(jnp.int32), slot_token, slot_gate, tile_expert, n_used.reshape(1).astype(jnp.int32)


def _moe_layer(x, g, mod, rw_pad, w1, w3, w2, geo):
    m, d = x.shape
    rt = d // LANES
    n_exp = w1.shape[0]
    tm = 512
    n_tiles = -(-(TOP_K * m) // tm) + n_exp
    n_tiles = -(-(n_tiles * tm) // GATHER_ROWS) * GATHER_ROWS // tm
    h, route = _router(x, g, mod, rw_pad, n_exp, geo)
    pos, slot_token, slot_gate, tile_expert, n_used = _moe_plan(route, n_exp, tm, n_tiles)
    (hs,) = _gather_rows(h, [slot_token], rt)
    u = _moe_in(hs, w1, w3, tile_expert, n_used, tm)
    y = _moe_out(u, w2, slot_gate.reshape(-1, 1), tile_expert, n_used, tm)
    mp = -(-m // GATHER_ROWS) * GATHER_ROWS
    pos2 = jnp.pad(pos.reshape(m, TOP_K), ((0, mp - m), (0, 0)))
    ya, yb = _gather_rows(y, [pos2[:, 0], pos2[:, 1]], rt)
    return _moe_combine(x, ya[:m * rt], yb[:m * rt], mod, geo)


def _final_norm_kernel(x_ref, g_ref, o_ref):
    x = x_ref[...]
    ms = jnp.mean(x * x, axis=-1, keepdims=True)
    o_ref[...] = (x * lax.rsqrt(ms + EPS)) * g_ref[...]


def _final_norm(x3, g, ctx_len):
    b, t, d = x3.shape
    seq = t - ctx_len
    off = ctx_len // SUB
    return pl.pallas_call(
        _final_norm_kernel,
        grid=(b, seq // SUB),
        in_specs=[pl.BlockSpec((None, SUB, d), lambda bi, i: (bi, off + i, 0)),
                  pl.BlockSpec((1, d), lambda bi, i: (0, 0))],
        out_specs=pl.BlockSpec((None, SUB, d), lambda bi, i: (bi, i, 0)),
        out_shape=jax.ShapeDtypeStruct((b, seq, d), F32),
        compiler_params=_params(("parallel", "parallel")),
        name="final_norm",
    )(x3, g.reshape(1, d))


def _rope_tables(batch, seq, ctx_len):
    rows = seq // GRID_W
    row_pos = jnp.repeat(jnp.arange(rows, dtype=F32), GRID_W)
    col_pos = jnp.tile(jnp.arange(GRID_W, dtype=F32), rows)
    inv_freq = ROPE_THETA ** (-jnp.arange(ROPE_PAIRS_PER_AXIS, dtype=F32) / ROPE_PAIRS_PER_AXIS)
    ang = jnp.concatenate([row_pos[:, None] * inv_freq, col_pos[:, None] * inv_freq], axis=-1)
    cos = jnp.cos(ang)
    sin = jnp.sin(ang)
    zero = jnp.zeros_like(sin)
    cos128 = jnp.tile(cos, (1, 4))
    sa128 = jnp.tile(jnp.concatenate([-sin, zero], axis=-1), (1, 2))
    sb128 = jnp.tile(jnp.concatenate([zero, sin], axis=-1), (1, 2))

    def stream(tab, ctx_val):
        ctx_rows = jnp.full((ctx_len, LANES), ctx_val, F32)
        return jnp.tile(jnp.concatenate([ctx_rows, tab], axis=0), (batch, 1))

    return stream(cos128, 1.0), stream(sa128, 0.0), stream(sb128, 0.0)


def _pad_w_in(w_in):
    depth, d, _ = w_in.shape
    main = w_in[:, :, :COL_DT]
    dt = w_in[:, :, COL_DT:COL_DT + 2 * SSM_HEADS].reshape(depth, d, 2 * SSM_GROUPS, SSM_HEADS_PER_GROUP)
    dt = jnp.pad(dt, ((0, 0), (0, 0), (0, 0), (0, LANES - SSM_HEADS_PER_GROUP)))
    return jnp.concatenate([main, dt.reshape(depth, d, 2 * SSM_GROUPS * LANES)], axis=-1).astype(BF16)


def _lane_rows(v):
    depth = v.shape[0]
    v = v.reshape(depth, 2 * SSM_GROUPS, 1, SSM_HEADS_PER_GROUP)
    return jnp.pad(v, ((0, 0), (0, 0), (0, 0), (0, LANES - SSM_HEADS_PER_GROUP)))


def kernel(x, c, ctx, c_ctx, w_mod, b_mod, norm_mix_g, w_in, lambda_q1, lambda_k1, lambda_q2, lambda_k2, subln_g, conv_w, conv_b, a_log, dt_bias, d_skip, ssm_norm_g, w_out, norm_ffn_g, ffn_w1, ffn_w3, ffn_w2, router_w, moe_w1, moe_w3, moe_w2, final_g):
    batch, seq, d = x.shape
    ctx_len = ctx.shape[1]
    depth = w_mod.shape[0]
    t = ctx_len + seq
    m = batch * t
    assert ctx_len % SUB == 0 and seq % SUB == 0 and seq % GRID_W == 0 and batch < 8
    geo = (t // SUB, ctx_len // SUB, batch)

    w_in_p = _pad_w_in(w_in)
    w_out_b = w_out.astype(BF16)
    ffn_w1_b, ffn_w3_b, ffn_w2_b = ffn_w1.astype(BF16), ffn_w3.astype(BF16), ffn_w2.astype(BF16)
    moe_w1_b, moe_w3_b, moe_w2_b = moe_w1.astype(BF16), moe_w3.astype(BF16), moe_w2.astype(BF16)
    n_exp = router_w.shape[-1]
    rw_pad = jnp.pad(router_w, ((0, 0), (0, 0), (0, LANES - n_exp))).astype(BF16)

    c_all = jnp.zeros((8, d), F32).at[:batch].set(c).at[batch].set(c_ctx)
    mod_all = _mod_table(c_all, w_mod, b_mod)
    cos, sa, sb = _rope_tables(batch, seq, ctx_len)
    bias_rows = _lane_rows(dt_bias)
    alog_rows = _lane_rows(a_log)
    dskip_rows = jnp.repeat(d_skip, SSM_HEAD_DIM, axis=-1).reshape(depth, SSM_GROUPS, 1, GROUP_WIDTH)

    xs = jnp.concatenate([ctx, x], axis=1).reshape(m, d)
    for li in range(depth):
        lambda_init = 0.8 - 0.6 * math.exp(-0.3 * li)
        mod = mod_all[li]
        lam = (jnp.exp(jnp.sum(lambda_q1[li] * lambda_k1[li])) - jnp.exp(jnp.sum(lambda_q2[li] * lambda_k2[li]))
               + lambda_init).reshape(1).astype(F32)

        p = _in_proj(xs, norm_mix_g[li], mod, w_in_p[li], cos, sa, sb, geo)
        p3 = p.reshape(batch, t, IN_COLS_PAD)
        att = _attention(p3, lam, subln_g[li], lambda_init, ctx_len)
        u3 = _conv_silu(p3, conv_w[li], conv_b[li], ctx_len)
        yf, yb = _ssd_scan(u3, p3, bias_rows[li], alog_rows[li], dskip_rows[li], ctx_len)
        ssm = _ssd_finish(yf.reshape(m, SSM_WIDTH), yb.reshape(m, SSM_WIDTH), p, ssm_norm_g[li])
        xs = _mm_resid([att.reshape(m, ATT_WIDTH), ssm], [w_out_b[li, :ATT_WIDTH], w_out_b[li, ATT_WIDTH:]],
                       xs, mod, 2, geo, tn=512)

        j = li // 2
        if li % 2 == 0:
            uu = _ffn_in(xs, norm_ffn_g[li], mod, ffn_w1_b[j], ffn_w3_b[j], geo)
            xs = _mm_resid([uu], [ffn_w2_b[j]], xs, mod, 5, geo, tn=256)
        else:
            xs = _moe_layer(xs, norm_ffn_g[li], mod, rw_pad[j], moe_w1_b[j], moe_w3_b[j], moe_w2_b[j], geo)
    return _final_norm(xs.reshape(batch, t, d), final_g, ctx_len)
```

```python
import functools
import math

import jax
import jax.numpy as jnp
from jax import lax
from jax.experimental import pallas as pl
from jax.experimental.pallas import tpu as pltpu

F32 = jnp.float32
BF16 = jnp.bfloat16

GRID_W = 64
EPS = 1e-6
N_MOD = 6
ATT_HEADS = 8
ATT_QK_DIM = 64
ATT_V_DIM = 128
ATT_WIDTH = ATT_HEADS * ATT_V_DIM
ROPE_THETA = 10000.0
ROPE_PAIRS_PER_AXIS = ATT_QK_DIM // 4
SSM_HEAD_DIM = 64
SSM_GROUPS = 2
SSM_HEADS_PER_GROUP = 8
SSM_HEADS = SSM_GROUPS * SSM_HEADS_PER_GROUP
SSM_WIDTH = SSM_HEADS * SSM_HEAD_DIM
SSM_STATE = 128
SSM_CONV = 5
SSM_CHUNK = 128
GROUP_WIDTH = SSM_WIDTH // SSM_GROUPS
XBC_COLS = SSM_WIDTH + 2 * SSM_GROUPS * SSM_STATE
TOP_K = 2

LANES = 128
SUB = 256
VMEM_LIMIT = 52 * 1024 * 1024

COL_Q = 0
COL_K = COL_Q + ATT_WIDTH
COL_V = COL_K + ATT_WIDTH
COL_Z = COL_V + ATT_WIDTH
COL_XBC = COL_Z + SSM_WIDTH
COL_DT = COL_XBC + XBC_COLS
IN_COLS_PAD = COL_DT + 2 * SSM_GROUPS * LANES


def _pick(n, cands):
    for c in cands:
        if n % c == 0:
            return c
    raise ValueError(f"no tile in {cands} divides {n}")


def _params(sem, vmem=VMEM_LIMIT, **kw):
    return pltpu.CompilerParams(dimension_semantics=sem, vmem_limit_bytes=vmem, **kw)


def _mod_row(sub_idx, geo):
    subs_per_batch, ctx_subs, n_batch = geo
    b = sub_idx // subs_per_batch
    w = sub_idx % subs_per_batch
    return jnp.where(w < ctx_subs, n_batch, b)


def _normmod(x, g, shift, scale):
    ms = jnp.mean(x * x, axis=-1, keepdims=True)
    hn = (x * lax.rsqrt(ms + EPS)) * g
    return hn * (1.0 + scale) + shift


def _silu(x):
    return x * (1.0 / (1.0 + jnp.exp(-x)))


def _mod_kernel(c_ref, w_ref, b_ref, o_ref):
    s = _silu(c_ref[...]).astype(BF16)
    o_ref[...] = jnp.dot(s, w_ref[...].astype(BF16), preferred_element_type=F32) + b_ref[...]


def _mod_table(c_all, w_mod, b_mod):
    depth, d, n = w_mod.shape
    tn = _pick(n, (1024, 512, 256, 128))
    return pl.pallas_call(
        _mod_kernel,
        grid=(depth, n // tn),
        in_specs=[
            pl.BlockSpec((8, d), lambda l, j: (0, 0)),
            pl.BlockSpec((None, d, tn), lambda l, j: (l, 0, j)),
            pl.BlockSpec((None, 1, tn), lambda l, j: (l, 0, j)),
        ],
        out_specs=pl.BlockSpec((None, 8, tn), lambda l, j: (l, 0, j)),
        out_shape=jax.ShapeDtypeStruct((depth, 8, n), F32),
        compiler_params=_params(("parallel", "parallel")),
        name="mod_table",
    )(c_all, w_mod, b_mod.reshape(depth, 1, n))


def _in_proj_kernel(x_ref, g_ref, mod_ref, w_ref, cos_ref, sa_ref, sb_ref, o_ref, h_ref, *, tm, tn, d, geo):
    i = pl.program_id(0)
    j = pl.program_id(1)

    @pl.when(j == 0)
    def _():
        for u in range(tm // SUB):
            row = _mod_row(i * (tm // SUB) + u, geo)
            rs = slice(u * SUB, (u + 1) * SUB)
            shift = mod_ref[pl.ds(row, 1), 0:d]
            scale = mod_ref[pl.ds(row, 1), d:2 * d]
            h_ref[rs, :] = _normmod(x_ref[rs, :], g_ref[...], shift, scale).astype(BF16)

    acc = jnp.dot(h_ref[...], w_ref[...], preferred_element_type=F32)

    def rope(scale):
        outs = []
        for hs in range(tn // LANES):
            a = acc[:, hs * LANES:(hs + 1) * LANES]
            r = (a * cos_ref[...] + pltpu.roll(a, LANES - 32, 1) * sa_ref[...]
                 + pltpu.roll(a, 32, 1) * sb_ref[...])
            outs.append(r * scale)
        return jnp.concatenate(outs, axis=1)

    @pl.when(j == COL_Q // tn)
    def _():
        o_ref[...] = rope(ATT_QK_DIM ** -0.5 * math.log2(math.e)).astype(BF16)

    @pl.when(j == COL_K // tn)
    def _():
        o_ref[...] = rope(1.0).astype(BF16)

    @pl.when(j >= COL_V // tn)
    def _():
        o_ref[...] = acc.astype(BF16)


def _in_proj(x, g, mod, w, cos, sa, sb, geo):
    m, d = x.shape
    n = w.shape[1]
    tm = _pick(m, (1024, 512, 256))
    tn = ATT_WIDTH
    kern = functools.partial(_in_proj_kernel, tm=tm, tn=tn, d=d, geo=geo)
    return pl.pallas_call(
        kern,
        grid=(m // tm, n // tn),
        in_specs=[
            pl.BlockSpec((tm, d), lambda i, j: (i, 0)),
            pl.BlockSpec((1, d), lambda i, j: (0, 0)),
            pl.BlockSpec(mod.shape, lambda i, j: (0, 0)),
            pl.BlockSpec((d, tn), lambda i, j: (0, j)),
            pl.BlockSpec((tm, LANES), lambda i, j: (i, 0)),
            pl.BlockSpec((tm, LANES), lambda i, j: (i, 0)),
            pl.BlockSpec((tm, LANES), lambda i, j: (i, 0)),
        ],
        out_specs=pl.BlockSpec((tm, tn), lambda i, j: (i, j)),
        out_shape=jax.ShapeDtypeStruct((m, n), BF16),
        scratch_shapes=[pltpu.VMEM((tm, d), BF16)],
        compiler_params=_params(("parallel", "arbitrary")),
        name="in_proj",
    )(x, g.reshape(1, d), mod, w, cos, sa, sb)


def _attn_kernel(lam_ref, q_ref, k_ref, v_ref, g_ref, o_ref, s_ref, vt_ref, *, tq, tk, ctx_tiles, ctx_chunks,
                 all_chunks, out_scale):
    qi = pl.program_id(2)

    @pl.when(qi == 0)
    def _():
        for c in range(all_chunks):
            vt_ref[c] = v_ref[c * tk:(c + 1) * tk, :].astype(F32).T.astype(BF16)

    qt = q_ref[...].astype(F32).T
    row = lax.broadcasted_iota(jnp.int32, (LANES, tq), 0)
    q2t = jnp.concatenate([jnp.where(row < ATT_QK_DIM, qt, 0.0), jnp.where(row >= ATT_QK_DIM, qt, 0.0)],
                          axis=1).astype(BF16)

    def fold(x):
        return x.reshape(tk // 8, 8, x.shape[-1])

    def attend(n_chunks):
        mm = None
        for c in range(n_chunks):
            s = jnp.dot(k_ref[c * tk:(c + 1) * tk, :], q2t, preferred_element_type=F32)
            s_ref[c] = s
            sf = jnp.max(fold(s), axis=0)
            mm = sf if mm is None else jnp.maximum(mm, sf)
        m = jnp.max(mm, axis=0, keepdims=True)
        ps = None
        acc = None
        for c in range(n_chunks):
            p = jnp.exp2(s_ref[c] - m)
            pf = jnp.sum(fold(p), axis=0)
            ps = pf if ps is None else ps + pf
            pv = jnp.dot(vt_ref[c], p.astype(BF16), preferred_element_type=F32)
            acc = pv if acc is None else acc + pv
        ot = acc * (1.0 / jnp.sum(ps, axis=0, keepdims=True))
        o = (ot[:, :tq] - lam_ref[0] * ot[:, tq:]).T
        ms = jnp.mean(o * o, axis=-1, keepdims=True)
        o = (o * lax.rsqrt(ms + EPS)) * g_ref[...] * out_scale
        o_ref[...] = o.astype(BF16)

    @pl.when(qi < ctx_tiles)
    def _():
        attend(ctx_chunks)

    @pl.when(qi >= ctx_tiles)
    def _():
        attend(all_chunks)


def _attention(p3, lam, subln_g, lambda_init, ctx_len):
    b, t, _ = p3.shape
    tq = SUB
    tk = SUB
    kern = functools.partial(_attn_kernel, tq=tq, tk=tk, ctx_tiles=ctx_len // tq, ctx_chunks=ctx_len // tk,
                             all_chunks=t // tk, out_scale=1.0 - lambda_init)
    return pl.pallas_call(
        kern,
        grid=(b, ATT_HEADS, t // tq),
        in_specs=[
            pl.BlockSpec(memory_space=pltpu.SMEM),
            pl.BlockSpec((None, tq, LANES), lambda bi, h, qi: (bi, qi, COL_Q // LANES + h)),
            pl.BlockSpec((None, t, LANES), lambda bi, h, qi: (bi, 0, COL_K // LANES + h)),
            pl.BlockSpec((None, t, LANES), lambda bi, h, qi: (bi, 0, COL_V // LANES + h)),
            pl.BlockSpec((1, LANES), lambda bi, h, qi: (0, 0)),
        ],
        out_specs=pl.BlockSpec((None, tq, LANES), lambda bi, h, qi: (bi, qi, h)),
        out_shape=jax.ShapeDtypeStruct((b, t, ATT_WIDTH), BF16),
        scratch_shapes=[pltpu.VMEM((t // tk, tk, 2 * tq), F32), pltpu.VMEM((t // tk, LANES, tk), BF16)],
        compiler_params=_params(("parallel", "parallel", "arbitrary")),
        name="diff_attention",
    )(lam, p3, p3, p3, subln_g.reshape(1, LANES))


CONV_HALO = 16


def _conv_kernel(prev_ref, cur_ref, next_ref, w_ref, b_ref, o_ref, *, rb, ctx_blocks, n_blocks):
    i = pl.program_id(1)
    prev_ok = jnp.logical_and(i != 0, i != ctx_blocks)
    next_ok = jnp.logical_and(i != ctx_blocks - 1, i != n_blocks - 1)
    prev = jnp.where(prev_ok, prev_ref[...].astype(F32), 0.0)
    nxt = jnp.where(next_ok, next_ref[...].astype(F32), 0.0)
    xc = jnp.concatenate([prev, cur_ref[...].astype(F32), nxt], axis=0)
    pad = SSM_CONV // 2
    acc = b_ref[...] + w_ref[0:1, :] * xc[CONV_HALO - pad:CONV_HALO - pad + rb, :]
    for k in range(1, SSM_CONV):
        s = CONV_HALO - pad + k
        acc = acc + w_ref[k:k + 1, :] * xc[s:s + rb, :]
    o_ref[...] = _silu(acc).astype(BF16)


def _conv_silu(p3, conv_w, conv_b, ctx_len):
    b, t, _ = p3.shape
    rb = SUB
    tc = 512
    hb = rb // CONV_HALO
    n_blocks = t // rb
    c0 = COL_XBC // tc
    kern = functools.partial(_conv_kernel, rb=rb, ctx_blocks=ctx_len // rb, n_blocks=n_blocks)
    last_halo = t // CONV_HALO - 1
    return pl.pallas_call(
        kern,
        grid=(b, n_blocks, XBC_COLS // tc),
        in_specs=[
            pl.BlockSpec((None, CONV_HALO, tc), lambda bi, i, j: (bi, jnp.maximum(i * hb - 1, 0), c0 + j)),
            pl.BlockSpec((None, rb, tc), lambda bi, i, j: (bi, i, c0 + j)),
            pl.BlockSpec((None, CONV_HALO, tc), lambda bi, i, j: (bi, jnp.minimum((i + 1) * hb, last_halo), c0 + j)),
            pl.BlockSpec((SSM_CONV, tc), lambda bi, i, j: (0, j)),
            pl.BlockSpec((1, tc), lambda bi, i, j: (0, j)),
        ],
        out_specs=pl.BlockSpec((None, rb, tc), lambda bi, i, j: (bi, i, j)),
        out_shape=jax.ShapeDtypeStruct((b, t, XBC_COLS), BF16),
        compiler_params=_params(("parallel", "parallel", "parallel")),
        name="conv_silu",
    )(p3, p3, p3, conv_w, conv_b.reshape(1, XBC_COLS))


def _split_bf16(a, pieces):
    parts = []
    r = a
    for _ in range(pieces):
        p = r.astype(BF16)
        parts.append(p)
        r = r - p.astype(F32)
    return parts


def _dot_sel(a, sel, pieces):
    out = None
    for p in _split_bf16(a, pieces):
        t = jnp.dot(p, sel, preferred_element_type=F32)
        out = t if out is None else out + t
    return out


def _sel_dot(sel, a, pieces):
    out = None
    for p in _split_bf16(a, pieces):
        t = jnp.dot(sel, p, preferred_element_type=F32)
        out = t if out is None else out + t
    return out


def _ssd_chunk(xs, bm, cm, dtb, bias, alog, h_ref, tri, ex64, ex128, reverse):
    q = xs.shape[0]
    hp = SSM_HEADS_PER_GROUP
    dt = jax.nn.softplus(dtb.astype(F32) + bias)
    a = dt * (-jnp.exp(alog))
    acs = _sel_dot(tri, a, 3)
    acs_t = acs.T
    end = acs[0:1, :] if reverse else acs[q - 1:q, :]
    dt_e = _dot_sel(dt, ex64, 2)
    eacs_e = _dot_sel(jnp.exp(acs), ex64, 2)
    dte_e = _dot_sel(jnp.exp(end - acs), ex64, 2)
    colb = _dot_sel(acs, ex128, 3)
    xsf = xs.astype(F32)
    xdt = xsf * dt_e
    cb = lax.dot_general(cm, bm, (((1,), (1,)), ((), ())), preferred_element_type=F32)
    qi = lax.broadcasted_iota(jnp.int32, (q, q), 0)
    si = lax.broadcasted_iota(jnp.int32, (q, q), 1)
    mask = (si >= qi) if reverse else (si <= qi)
    lane = lax.broadcasted_iota(jnp.int32, (q, LANES), 1)
    ys = []
    for j in range(hp // 2):
        xp = xdt[:, j * LANES:(j + 1) * LANES]
        halves = (jnp.where(lane < SSM_HEAD_DIM, xp, 0.0).astype(BF16),
                  jnp.where(lane >= SSM_HEAD_DIM, xp, 0.0).astype(BF16))
        acc = None
        for k in range(2):
            r = 2 * j + k
            seg = colb[:, r * LANES:(r + 1) * LANES] - acs_t[r:r + 1, :]
            dec = jnp.exp(jnp.where(mask, seg, -1e30))
            t = jnp.dot((cb * dec).astype(BF16), halves[k], preferred_element_type=F32)
            acc = t if acc is None else acc + t
        ys.append(acc)
    y_diag = jnp.concatenate(ys, axis=1)
    h = h_ref[...]
    y_off = jnp.dot(cm, h.astype(BF16), preferred_element_type=F32) * eacs_e
    cdec = eacs_e[0:1, :] if reverse else eacs_e[q - 1:q, :]
    xd = (xdt * dte_e).astype(BF16)
    h_ref[...] = h * cdec + lax.dot_general(bm, xd, (((0,), (0,)), ((), ())), preferred_element_type=F32)
    return y_diag + y_off, xsf


def _ssd_kernel(xsf_ref, bmf_ref, cmf_ref, dtf_ref, xsb_ref, bmb_ref, cmb_ref, dtb_ref,
                biasf_ref, biasb_ref, alogf_ref, alogb_ref, dskip_ref, tril_ref, triu_ref, ex64_ref, ex128_ref,
                yf_ref, yb_ref, hf_ref, hb_ref):
    @pl.when(pl.program_id(2) == 0)
    def _():
        hf_ref[...] = jnp.zeros_like(hf_ref)
        hb_ref[...] = jnp.zeros_like(hb_ref)

    yf, xsf = _ssd_chunk(xsf_ref[...], bmf_ref[...], cmf_ref[...], dtf_ref[...], biasf_ref[...], alogf_ref[...],
                         hf_ref, tril_ref[...], ex64_ref[...], ex128_ref[...], False)
    yf_ref[...] = yf + dskip_ref[...] * xsf
    yb, _ = _ssd_chunk(xsb_ref[...], bmb_ref[...], cmb_ref[...], dtb_ref[...], biasb_ref[...], alogb_ref[...],
                       hb_ref, triu_ref[...], ex64_ref[...], ex128_ref[...], True)
    yb_ref[...] = yb


def _ssd_scan(u3, p3, bias4, alog4, dskip, ctx_len):
    b, t, _ = u3.shape
    q = SSM_CHUNK
    ns = t // q
    ncc = ctx_len // q
    g_n = SSM_GROUPS

    def fwd(s):
        return s

    def bwd(s):
        return jnp.where(s < ncc, ncc - 1 - s, ns - 1 + ncc - s)

    xs_blk = GROUP_WIDTH // LANES
    b_col = SSM_WIDTH // LANES
    c_col = b_col + g_n
    dt_col = COL_DT // LANES
    del xs_blk

    def specs(cidx, direction):
        return [
            pl.BlockSpec((None, q, GROUP_WIDTH), lambda bi, g, s: (bi, cidx(s), g)),
            pl.BlockSpec((None, q, LANES), lambda bi, g, s: (bi, cidx(s), b_col + g)),
            pl.BlockSpec((None, q, LANES), lambda bi, g, s: (bi, cidx(s), c_col + g)),
            pl.BlockSpec((None, q, LANES), lambda bi, g, s: (bi, cidx(s), dt_col + direction * g_n + g)),
        ]

    row = lambda direction: pl.BlockSpec((None, 1, LANES), lambda bi, g, s: (direction * g_n + g, 0, 0))
    full = lambda arr: pl.BlockSpec(arr.shape, lambda bi, g, s: (0,) * arr.ndim)

    ri = lax.broadcasted_iota(jnp.int32, (q, q), 0)
    ci = lax.broadcasted_iota(jnp.int32, (q, q), 1)
    tril = (ci <= ri).astype(BF16)
    triu = (ci >= ri).astype(BF16)
    r64 = lax.broadcasted_iota(jnp.int32, (LANES, GROUP_WIDTH), 0)
    c64 = lax.broadcasted_iota(jnp.int32, (LANES, GROUP_WIDTH), 1)
    ex64 = (c64 // SSM_HEAD_DIM == r64).astype(BF16)
    r128 = lax.broadcasted_iota(jnp.int32, (LANES, SSM_HEADS_PER_GROUP * LANES), 0)
    c128 = lax.broadcasted_iota(jnp.int32, (LANES, SSM_HEADS_PER_GROUP * LANES), 1)
    ex128 = (c128 // LANES == r128).astype(BF16)

    in_specs = (specs(fwd, 0) + specs(bwd, 1) + [row(0), row(1), row(0), row(1),
                pl.BlockSpec((None, 1, GROUP_WIDTH), lambda bi, g, s: (g, 0, 0)),
                full(tril), full(triu), full(ex64), full(ex128)])
    out_spec = lambda cidx: pl.BlockSpec((None, q, GROUP_WIDTH), lambda bi, g, s: (bi, cidx(s), g))
    return pl.pallas_call(
        _ssd_kernel,
        grid=(b, g_n, ns),
        in_specs=in_specs,
        out_specs=[out_spec(fwd), out_spec(bwd)],
        out_shape=[jax.ShapeDtypeStruct((b, t, SSM_WIDTH), F32)] * 2,
        scratch_shapes=[pltpu.VMEM((SSM_STATE, GROUP_WIDTH), F32)] * 2,
        compiler_params=_params(("parallel", "parallel", "arbitrary")),
        name="ssd_scan",
    )(u3, u3, u3, p3, u3, u3, u3, p3, bias4, bias4, alog4, alog4, dskip, tril, triu, ex64, ex128)


def _ssd_finish_kernel(yf_ref, yb_ref, z_ref, g_ref, o_ref):
    y = (yf_ref[...] + yb_ref[...]) * _silu(z_ref[...].astype(F32))
    outs = []
    for g in range(SSM_GROUPS):
        yg = y[:, g * GROUP_WIDTH:(g + 1) * GROUP_WIDTH]
        ms = jnp.mean(yg * yg, axis=-1, keepdims=True)
        outs.append((yg * lax.rsqrt(ms + EPS)) * g_ref[:, g * GROUP_WIDTH:(g + 1) * GROUP_WIDTH])
    o_ref[...] = jnp.concatenate(outs, axis=1).astype(BF16)


def _ssd_finish(yf, yb, p, norm_g):
    m = yf.shape[0]
    tm = _pick(m, (512, 256))
    return pl.pallas_call(
        _ssd_finish_kernel,
        grid=(m // tm,),
        in_specs=[
            pl.BlockSpec((tm, SSM_WIDTH), lambda i: (i, 0)),
            pl.BlockSpec((tm, SSM_WIDTH), lambda i: (i, 0)),
            pl.BlockSpec((tm, SSM_WIDTH), lambda i: (i, COL_Z // SSM_WIDTH)),
            pl.BlockSpec((1, SSM_WIDTH), lambda i: (0, 0)),
        ],
        out_specs=pl.BlockSpec((tm, SSM_WIDTH), lambda i: (i, 0)),
        out_shape=jax.ShapeDtypeStruct((m, SSM_WIDTH), BF16),
        compiler_params=_params(("parallel",)),
        name="ssd_finish",
    )(yf, yb, p, norm_g.reshape(1, SSM_WIDTH))


def _mm_resid_kernel(*refs, n_a, tm, geo):
    a_refs = refs[:n_a]
    w_refs = refs[n_a:2 * n_a]
    x_ref, gate_ref, o_ref = refs[2 * n_a:]
    i = pl.program_id(0)
    acc = None
    for a_ref, w_ref in zip(a_refs, w_refs):
        t = jnp.dot(a_ref[...], w_ref[...], preferred_element_type=F32)
        acc = t if acc is None else acc + t
    for u in range(tm // SUB):
        row = _mod_row(i * (tm // SUB) + u, geo)
        rs = slice(u * SUB, (u + 1) * SUB)
        o_ref[rs, :] = x_ref[rs, :] + gate_ref[pl.ds(row, 1), :] * acc[rs, :]


def _mm_resid(a_list, w_list, x, mod, gate_idx, geo, tn):
    m, d = x.shape
    tm = _pick(m, (1024, 512, 256))
    n_a = len(a_list)
    kern = functools.partial(_mm_resid_kernel, n_a=n_a, tm=tm, geo=geo)
    gate_blk = gate_idx * d // tn
    in_specs = ([pl.BlockSpec((tm, a.shape[1]), lambda i, j: (i, 0)) for a in a_list]
                + [pl.BlockSpec((w.shape[0], tn), lambda i, j: (0, j)) for w in w_list]
                + [pl.BlockSpec((tm, tn), lambda i, j: (i, j)),
                   pl.BlockSpec((mod.shape[0], tn), lambda i, j: (0, gate_blk + j))])
    return pl.pallas_call(
        kern,
        grid=(m // tm, d // tn),
        in_specs=in_specs,
        out_specs=pl.BlockSpec((tm, tn), lambda i, j: (i, j)),
        out_shape=jax.ShapeDtypeStruct((m, d), F32),
        compiler_params=_params(("parallel", "parallel")),
        name="matmul_gated_residual",
    )(*a_list, *w_list, x, mod)


def _ffn_in_kernel(x_ref, g_ref, mod_ref, w1_ref, w3_ref, o_ref, h_ref, *, tm, d, geo):
    i = pl.program_id(0)

    @pl.when(pl.program_id(1) == 0)
    def _():
        for u in range(tm // SUB):
            row = _mod_row(i * (tm // SUB) + u, geo)
            rs = slice(u * SUB, (u + 1) * SUB)
            shift = mod_ref[pl.ds(row, 1), 3 * d:4 * d]
            scale = mod_ref[pl.ds(row, 1), 4 * d:5 * d]
            h_ref[rs, :] = _normmod(x_ref[rs, :], g_ref[...], shift, scale).astype(BF16)

    h = h_ref[...]
    a = jnp.dot(h, w1_ref[...], preferred_element_type=F32)
    b = jnp.dot(h, w3_ref[...], preferred_element_type=F32)
    o_ref[...] = (_silu(a) * b).astype(BF16)


def _ffn_in(x, g, mod, w1, w3, geo):
    m, d = x.shape
    ff = w1.shape[1]
    tm = _pick(m, (1024, 512, 256))
    tf = _pick(ff, (512, 256, 128))
    kern = functools.partial(_ffn_in_kernel, tm=tm, d=d, geo=geo)
    return pl.pallas_call(
        kern,
        grid=(m // tm, ff // tf),
        in_specs=[
            pl.BlockSpec((tm, d), lambda i, j: (i, 0)),
            pl.BlockSpec((1, d), lambda i, j: (0, 0)),
            pl.BlockSpec(mod.shape, lambda i, j: (0, 0)),
            pl.BlockSpec((d, tf), lambda i, j: (0, j)),
            pl.BlockSpec((d, tf), lambda i, j: (0, j)),
        ],
        out_specs=pl.BlockSpec((tm, tf), lambda i, j: (i, j)),
        out_shape=jax.ShapeDtypeStruct((m, ff), BF16),
        scratch_shapes=[pltpu.VMEM((tm, d), BF16)],
        compiler_params=_params(("parallel", "arbitrary")),
        name="ffn_in",
    )(x, g.reshape(1, d), mod, w1, w3)


def _store_row_tiles(dst_ref, src_ref, rt):
    tm = src_ref.shape[0]
    for a in range(rt):
        dst_ref[pl.ds(a, tm, stride=rt), :] = src_ref[:, a * LANES:(a + 1) * LANES]


def _load_row_tiles(src_ref, row0, tm, rt):
    return jnp.concatenate([src_ref[pl.ds(row0 * rt + a, tm, stride=rt), :] for a in range(rt)], axis=1)


def _router_kernel(x_ref, g_ref, mod_ref, rw_ref, h_ref, r_ref, hs_ref, *, tm, d, n_exp, geo):
    i = pl.program_id(0)
    for u in range(tm // SUB):
        row = _mod_row(i * (tm // SUB) + u, geo)
        rs = slice(u * SUB, (u + 1) * SUB)
        shift = mod_ref[pl.ds(row, 1), 3 * d:4 * d]
        scale = mod_ref[pl.ds(row, 1), 4 * d:5 * d]
        hs_ref[rs, :] = _normmod(x_ref[rs, :], g_ref[...], shift, scale)
    _store_row_tiles(h_ref, hs_ref, d // LANES)
    logits = jnp.dot(hs_ref[...].astype(BF16), rw_ref[...], preferred_element_type=F32)
    lane = lax.broadcasted_iota(jnp.int32, logits.shape, 1)
    neg = jnp.float32(-1e30)
    logits = jnp.where(lane < n_exp, logits, neg)
    m1 = jnp.max(logits, axis=-1, keepdims=True)
    i1 = jnp.min(jnp.where(logits == m1, lane, LANES), axis=-1, keepdims=True)
    rest = jnp.where(lane == i1, neg, logits)
    m2 = jnp.max(rest, axis=-1, keepdims=True)
    i2 = jnp.min(jnp.where(rest == m2, lane, LANES), axis=-1, keepdims=True)
    t = jnp.exp(m2 - m1)
    g1 = 1.0 / (1.0 + t)
    g2 = t * g1
    out = jnp.where(lane == 0, i1.astype(F32),
                    jnp.where(lane == 1, i2.astype(F32),
                              jnp.where(lane == 2, g1, jnp.where(lane == 3, g2, 0.0))))
    r_ref[...] = out


def _router(x, g, mod, rw_pad, n_exp, geo):
    m, d = x.shape
    rt = d // LANES
    tm = _pick(m, (512, 256))
    kern = functools.partial(_router_kernel, tm=tm, d=d, n_exp=n_exp, geo=geo)
    return pl.pallas_call(
        kern,
        grid=(m // tm,),
        in_specs=[
            pl.BlockSpec((tm, d), lambda i: (i, 0)),
            pl.BlockSpec((1, d), lambda i: (0, 0)),
            pl.BlockSpec(mod.shape, lambda i: (0, 0)),
            pl.BlockSpec((d, LANES), lambda i: (0, 0)),
        ],
        out_specs=[pl.BlockSpec((tm * rt, LANES), lambda i: (i, 0)), pl.BlockSpec((tm, LANES), lambda i: (i, 0))],
        out_shape=[jax.ShapeDtypeStruct((m * rt, LANES), F32), jax.ShapeDtypeStruct((m, LANES), F32)],
        scratch_shapes=[pltpu.VMEM((tm, d), F32)],
        compiler_params=_params(("parallel",)),
        name="moe_router",
    )(x, g.reshape(1, d), mod, rw_pad)


GATHER_ROWS = 512


def _gather_kernel(*refs, n_out, rows, rt):
    idx_refs = refs[:n_out]
    src_ref = refs[n_out]
    dst_refs = refs[n_out + 1:2 * n_out + 1]
    sem = refs[2 * n_out + 1]

    def copy(k, r):
        src_row = pl.multiple_of(idx_refs[k][r] * rt, rt)
        dst_row = pl.multiple_of(r * rt, rt)
        return pltpu.make_async_copy(src_ref.at[pl.ds(src_row, rt), :], dst_refs[k].at[pl.ds(dst_row, rt), :], sem)

    def start(r, c):
        for k in range(n_out):
            copy(k, r).start()
        return c

    def wait(r, c):
        for k in range(n_out):
            copy(k, r).wait()
        return c

    lax.fori_loop(0, rows, start, 0)
    lax.fori_loop(0, rows, wait, 0)


def _gather_rows(src, idx_list, rt):
    n = idx_list[0].shape[0]
    rows = GATHER_ROWS
    assert n % rows == 0
    n_out = len(idx_list)
    kern = functools.partial(_gather_kernel, n_out=n_out, rows=rows, rt=rt)
    outs = pl.pallas_call(
        kern,
        grid=(n // rows,),
        in_specs=([pl.BlockSpec((rows,), lambda i: (i,), memory_space=pltpu.SMEM) for _ in idx_list]
                  + [pl.BlockSpec(memory_space=pl.ANY)]),
        out_specs=[pl.BlockSpec((rows * rt, LANES), lambda i: (i, 0)) for _ in idx_list],
        out_shape=[jax.ShapeDtypeStruct((n * rt, LANES), src.dtype) for _ in idx_list],
        scratch_shapes=[pltpu.SemaphoreType.DMA(())],
        compiler_params=_params(("arbitrary",), has_side_effects=True),
        name="gather_rows",
    )(*idx_list, src)
    return outs


def _moe_in_kernel(te_ref, nu_ref, h_ref, w1_ref, w3_ref, o_ref, hb_ref):
    t = pl.program_id(0)
    j = pl.program_id(1)
    tm, d = hb_ref.shape

    @pl.when(t < nu_ref[0])
    def _():
        @pl.when(j == 0)
        def _():
            hb_ref[...] = _load_row_tiles(h_ref, 0, tm, d // LANES).astype(BF16)

        h = hb_ref[...]
        a = jnp.dot(h, w1_ref[...], preferred_element_type=F32)
        b = jnp.dot(h, w3_ref[...], preferred_element_type=F32)
        o_ref[...] = (_silu(a) * b).astype(BF16)

    @pl.when(t >= nu_ref[0])
    def _():
        o_ref[...] = jnp.zeros_like(o_ref)


def _moe_in(hs, w1, w3, tile_expert, n_used, tm):
    d = w1.shape[1]
    rt = d // LANES
    s = hs.shape[0] // rt
    ff = w1.shape[2]
    tf = _pick(ff, (512, 256, 128))
    n_ff = ff // tf

    def w_map(t, j, te, nu):
        return (te[t], 0, jnp.where(t < nu[0], j, n_ff - 1))

    return pl.pallas_call(
        _moe_in_kernel,
        grid_spec=pltpu.PrefetchScalarGridSpec(
            num_scalar_prefetch=2,
            grid=(s // tm, n_ff),
            in_specs=[
                pl.BlockSpec((tm * rt, LANES), lambda t, j, te, nu: (t, 0)),
                pl.BlockSpec((None, d, tf), w_map),
                pl.BlockSpec((None, d, tf), w_map),
            ],
            out_specs=pl.BlockSpec((tm, tf), lambda t, j, te, nu: (t, j)),
            scratch_shapes=[pltpu.VMEM((tm, d), BF16)],
        ),
        out_shape=jax.ShapeDtypeStruct((s, ff), BF16),
        compiler_params=_params(("arbitrary", "arbitrary")),
        name="moe_experts_in",
    )(tile_expert, n_used, hs, w1, w3)


def _moe_out_kernel(te_ref, nu_ref, u_ref, w2_ref, gate_ref, o_ref, acc_ref, *, n_k):
    t = pl.program_id(0)
    k = pl.program_id(1)
    valid = t < nu_ref[0]

    @pl.when(valid)
    def _():
        part = jnp.dot(u_ref[...], w2_ref[...], preferred_element_type=F32)

        @pl.when(k == 0)
        def _():
            acc_ref[...] = part

        @pl.when(k > 0)
        def _():
            acc_ref[...] += part

        @pl.when(k == n_k - 1)
        def _():
            acc_ref[...] = gate_ref[...] * acc_ref[...]
            _store_row_tiles(o_ref, acc_ref, acc_ref.shape[1] // LANES)

    @pl.when(jnp.logical_and(jnp.logical_not(valid), k == n_k - 1))
    def _():
        o_ref[...] = jnp.zeros_like(o_ref)


def _moe_out(u, w2, slot_gate, tile_expert, n_used, tm):
    s, ff = u.shape
    d = w2.shape[2]
    rt = d // LANES
    tk = _pick(ff, (1408, 1024, 512, 256, 128))
    n_k = ff // tk

    def k_idx(t, k, nu):
        return jnp.where(t < nu[0], k, n_k - 1)

    return pl.pallas_call(
        functools.partial(_moe_out_kernel, n_k=n_k),
        grid_spec=pltpu.PrefetchScalarGridSpec(
            num_scalar_prefetch=2,
            grid=(s // tm, n_k),
            in_specs=[
                pl.BlockSpec((tm, tk), lambda t, k, te, nu: (t, k_idx(t, k, nu))),
                pl.BlockSpec((None, tk, d), lambda t, k, te, nu: (te[t], k_idx(t, k, nu), 0)),
                pl.BlockSpec((tm, 1), lambda t, k, te, nu: (t, 0)),
            ],
            out_specs=pl.BlockSpec((tm * rt, LANES), lambda t, k, te, nu: (t, 0)),
            scratch_shapes=[pltpu.VMEM((tm, d), F32)],
        ),
        out_shape=jax.ShapeDtypeStruct((s * rt, LANES), F32),
        compiler_params=_params(("arbitrary", "arbitrary")),
        name="moe_experts_out",
    )(tile_expert, n_used, u, w2, slot_gate)


def _moe_combine_kernel(x_ref, ya_ref, yb_ref, gate_ref, o_ref, *, tm, geo):
    i = pl.program_id(0)
    rt = x_ref.shape[1] // LANES
    for u in range(tm // SUB):
        row = _mod_row(i * (tm // SUB) + u, geo)
        rs = slice(u * SUB, (u + 1) * SUB)
        y = _load_row_tiles(ya_ref, u * SUB, SUB, rt) + _load_row_tiles(yb_ref, u * SUB, SUB, rt)
        o_ref[rs, :] = x_ref[rs, :] + gate_ref[pl.ds(row, 1), :] * y


def _moe_combine(x, ya, yb, mod, geo):
    m, d = x.shape
    rt = d // LANES
    tm = _pick(m, (512, 256))
    kern = functools.partial(_moe_combine_kernel, tm=tm, geo=geo)
    blk = pl.BlockSpec((tm, d), lambda i: (i, 0))
    blk3 = pl.BlockSpec((tm * rt, LANES), lambda i: (i, 0))
    return pl.pallas_call(
        kern,
        grid=(m // tm,),
        in_specs=[blk, blk3, blk3, pl.BlockSpec((mod.shape[0], d), lambda i: (0, 5))],
        out_specs=blk,
        out_shape=jax.ShapeDtypeStruct((m, d), F32),
        compiler_params=_params(("parallel",)),
        name="moe_combine",
    )(x, ya, yb, mod)


def _moe_plan(route, n_exp, tm, n_tiles):
    m = route.shape[0]
    e = route[:, 0:TOP_K].astype(jnp.int32).reshape(-1)
    gate = route[:, TOP_K:2 * TOP_K].reshape(-1)
    onehot = (e[:, None] == jnp.arange(n_exp, dtype=jnp.int32)[None, :]).astype(jnp.int32)
    csum = jnp.cumsum(onehot, axis=0)
    rank = jnp.sum(csum * onehot, axis=1) - 1
    counts = csum[-1]
    tiles_per = (counts + tm - 1) // tm
    tile_end = jnp.cumsum(tiles_per)
    tile_start = tile_end - tiles_per
    pos = tile_start[e] * tm + rank
    n_used = tile_end[-1]
    n_slots = n_tiles * tm
    token = jnp.arange(TOP_K * m, dtype=jnp.int32) // TOP_K
    slot_token = jnp.zeros((n_slots,), jnp.int32).at[pos].set(token)
    slot_gate = jnp.zeros((n_slots,), F32).at[pos].set(gate)
    tile_ids = jnp.arange(n_tiles, dtype=jnp.int32)
    tile_expert = jnp.sum((tile_ids[:, None] >= tile_end[None, :]).astype(jnp.int32), axis=1)
    last_expert = jnp.sum((n_used - 1 >= tile_end).astype(jnp.int32))
    tile_expert = jnp.where(tile_ids < n_used, tile_expert, last_expert).astype(jnp.int32)
    return pos.astype(jnp.int32), slot_token, slot_gate, tile_expert, n_used.reshape(1).astype(jnp.int32)


def _moe_layer(x, g, mod, rw_pad, w1, w3, w2, geo):
    m, d = x.shape
    rt = d // LANES
    n_exp = w1.shape[0]
    tm = 512
    n_tiles = -(-(TOP_K * m) // tm) + n_exp
    n_tiles = -(-(n_tiles * tm) // GATHER_ROWS) * GATHER_ROWS // tm
    h, route = _router(x, g, mod, rw_pad, n_exp, geo)
    pos, slot_token, slot_gate, tile_expert, n_used = _moe_plan(route, n_exp, tm, n_tiles)
    (hs,) = _gather_rows(h, [slot_token], rt)
    u = _moe_in(hs, w1, w3, tile_expert, n_used, tm)
    y = _moe_out(u, w2, slot_gate.reshape(-1, 1), tile_expert, n_used, tm)
    mp = -(-m // GATHER_ROWS) * GATHER_ROWS
    pos2 = jnp.pad(pos.reshape(m, TOP_K), ((0, mp - m), (0, 0)))
    ya, yb = _gather_rows(y, [pos2[:, 0], pos2[:, 1]], rt)
    return _moe_combine(x, ya[:m * rt], yb[:m * rt], mod, geo)


def _final_norm_kernel(x_ref, g_ref, o_ref):
    x = x_ref[...]
    ms = jnp.mean(x * x, axis=-1, keepdims=True)
    o_ref[...] = (x * lax.rsqrt(ms + EPS)) * g_ref[...]


def _final_norm(x3, g, ctx_len):
    b, t, d = x3.shape
    seq = t - ctx_len
    off = ctx_len // SUB
    return pl.pallas_call(
        _final_norm_kernel,
        grid=(b, seq // SUB),
        in_specs=[pl.BlockSpec((None, SUB, d), lambda bi, i: (bi, off + i, 0)),
                  pl.BlockSpec((1, d), lambda bi, i: (0, 0))],
        out_specs=pl.BlockSpec((None, SUB, d), lambda bi, i: (bi, i, 0)),
        out_shape=jax.ShapeDtypeStruct((b, seq, d), F32),
        compiler_params=_params(("parallel", "parallel")),
        name="final_norm",
    )(x3, g.reshape(1, d))


def _rope_tables(batch, seq, ctx_len):
    rows = seq // GRID_W
    row_pos = jnp.repeat(jnp.arange(rows, dtype=F32), GRID_W)
    col_pos = jnp.tile(jnp.arange(GRID_W, dtype=F32), rows)
    inv_freq = ROPE_THETA ** (-jnp.arange(ROPE_PAIRS_PER_AXIS, dtype=F32) / ROPE_PAIRS_PER_AXIS)
    ang = jnp.concatenate([row_pos[:, None] * inv_freq, col_pos[:, None] * inv_freq], axis=-1)
    cos = jnp.cos(ang)
    sin = jnp.sin(ang)
    zero = jnp.zeros_like(sin)
    cos128 = jnp.tile(cos, (1, 4))
    sa128 = jnp.tile(jnp.concatenate([-sin, zero], axis=-1), (1, 2))
    sb128 = jnp.tile(jnp.concatenate([zero, sin], axis=-1), (1, 2))

    def stream(tab, ctx_val):
        ctx_rows = jnp.full((ctx_len, LANES), ctx_val, F32)
        return jnp.tile(jnp.concatenate([ctx_rows, tab], axis=0), (batch, 1))

    return stream(cos128, 1.0), stream(sa128, 0.0), stream(sb128, 0.0)


def _pad_w_in(w_in):
    depth, d, _ = w_in.shape
    main = w_in[:, :, :COL_DT]
    dt = w_in[:, :, COL_DT:COL_DT + 2 * SSM_HEADS].reshape(depth, d, 2 * SSM_GROUPS, SSM_HEADS_PER_GROUP)
    dt = jnp.pad(dt, ((0, 0), (0, 0), (0, 0), (0, LANES - SSM_HEADS_PER_GROUP)))
    return jnp.concatenate([main, dt.reshape(depth, d, 2 * SSM_GROUPS * LANES)], axis=-1).astype(BF16)


def _lane_rows(v):
    depth = v.shape[0]
    v = v.reshape(depth, 2 * SSM_GROUPS, 1, SSM_HEADS_PER_GROUP)
    return jnp.pad(v, ((0, 0), (0, 0), (0, 0), (0, LANES - SSM_HEADS_PER_GROUP)))


def kernel(x, c, ctx, c_ctx, w_mod, b_mod, norm_mix_g, w_in, lambda_q1, lambda_k1, lambda_q2, lambda_k2, subln_g, conv_w, conv_b, a_log, dt_bias, d_skip, ssm_norm_g, w_out, norm_ffn_g, ffn_w1, ffn_w3, ffn_w2, router_w, moe_w1, moe_w3, moe_w2, final_g):
    batch, seq, d = x.shape
    ctx_len = ctx.shape[1]
    depth = w_mod.shape[0]
    t = ctx_len + seq
    m = batch * t
    assert ctx_len % SUB == 0 and seq % SUB == 0 and seq % GRID_W == 0 and batch < 8
    geo = (t // SUB, ctx_len // SUB, batch)

    w_in_p = _pad_w_in(w_in)
    w_out_b = w_out.astype(BF16)
    n_exp = router_w.shape[-1]
    rw_pad = jnp.pad(router_w, ((0, 0), (0, 0), (0, LANES - n_exp))).astype(BF16)

    c_all = jnp.zeros((8, d), F32).at[:batch].set(c).at[batch].set(c_ctx)
    mod_all = _mod_table(c_all, w_mod, b_mod)
    cos, sa, sb = _rope_tables(batch, seq, ctx_len)
    bias_rows = _lane_rows(dt_bias)
    alog_rows = _lane_rows(a_log)
    dskip_rows = jnp.repeat(d_skip, SSM_HEAD_DIM, axis=-1).reshape(depth, SSM_GROUPS, 1, GROUP_WIDTH)

    xs = jnp.concatenate([ctx, x], axis=1).reshape(m, d)
    for li in range(depth):
        lambda_init = 0.8 - 0.6 * math.exp(-0.3 * li)
        mod = mod_all[li]
        lam = (jnp.exp(jnp.sum(lambda_q1[li] * lambda_k1[li])) - jnp.exp(jnp.sum(lambda_q2[li] * lambda_k2[li]))
               + lambda_init).reshape(1).astype(F32)

        p = _in_proj(xs, norm_mix_g[li], mod, w_in_p[li], cos, sa, sb, geo)
        p3 = p.reshape(batch, t, IN_COLS_PAD)
        att = _attention(p3, lam, subln_g[li], lambda_init, ctx_len)
        u3 = _conv_silu(p3, conv_w[li], conv_b[li], ctx_len)
        yf, yb = _ssd_scan(u3, p3, bias_rows[li], alog_rows[li], dskip_rows[li], ctx_len)
        ssm = _ssd_finish(yf.reshape(m, SSM_WIDTH), yb.reshape(m, SSM_WIDTH), p, ssm_norm_g[li])
        xs = _mm_resid([att.reshape(m, ATT_WIDTH), ssm], [w_out_b[li, :ATT_WIDTH], w_out_b[li, ATT_WIDTH:]],
                       xs, mod, 2, geo, tn=512)

        j = li // 2
        if li % 2 == 0:
            uu = _ffn_in(xs, norm_ffn_g[li], mod, ffn_w1[j].astype(BF16), ffn_w3[j].astype(BF16), geo)
            xs = _mm_resid([uu], [ffn_w2[j].astype(BF16)], xs, mod, 5, geo, tn=256)
        else:
            xs = _moe_layer(xs, norm_ffn_g[li], mod, rw_pad[j], moe_w1[j].astype(BF16), moe_w3[j].astype(BF16),
                            moe_w2[j].astype(BF16), geo)
    return _final_norm(xs.reshape(batch, t, d), final_g, ctx_len)
```

```python
import functools
import math

import jax
import jax.numpy as jnp
from jax import lax
from jax.experimental import pallas as pl
from jax.experimental.pallas import tpu as pltpu

F32 = jnp.float32
BF16 = jnp.bfloat16

GRID_W = 64
EPS = 1e-6
N_MOD = 6
ATT_HEADS = 8
ATT_QK_DIM = 64
ATT_V_DIM = 128
ATT_WIDTH = ATT_HEADS * ATT_V_DIM
ROPE_THETA = 10000.0
ROPE_PAIRS_PER_AXIS = ATT_QK_DIM // 4
SSM_HEAD_DIM = 64
SSM_GROUPS = 2
SSM_HEADS_PER_GROUP = 8
SSM_HEADS = SSM_GROUPS * SSM_HEADS_PER_GROUP
SSM_WIDTH = SSM_HEADS * SSM_HEAD_DIM
SSM_STATE = 128
SSM_CONV = 5
SSM_CHUNK = 128
GROUP_WIDTH = SSM_WIDTH // SSM_GROUPS
XBC_COLS = SSM_WIDTH + 2 * SSM_GROUPS * SSM_STATE
TOP_K = 2

LANES = 128
SUB = 256
ATT_HEADS_PER_STEP = 2
VMEM_LIMIT = 52 * 1024 * 1024

COL_Q = 0
COL_K = COL_Q + ATT_WIDTH
COL_V = COL_K + ATT_WIDTH
COL_Z = COL_V + ATT_WIDTH
COL_XBC = COL_Z + SSM_WIDTH
COL_DT = COL_XBC + XBC_COLS
IN_COLS_PAD = COL_DT + 2 * SSM_GROUPS * LANES


def _pick(n, cands):
    for c in cands:
        if n % c == 0:
            return c
    raise ValueError(f"no tile in {cands} divides {n}")


def _params(sem, vmem=VMEM_LIMIT, **kw):
    return pltpu.CompilerParams(dimension_semantics=sem, vmem_limit_bytes=vmem, **kw)


def _mod_row(sub_idx, geo):
    subs_per_batch, ctx_subs, n_batch = geo
    b = sub_idx // subs_per_batch
    w = sub_idx % subs_per_batch
    return jnp.where(w < ctx_subs, n_batch, b)


def _normmod(x, g, shift, scale):
    ms = jnp.mean(x * x, axis=-1, keepdims=True)
    hn = (x * lax.rsqrt(ms + EPS)) * g
    return hn * (1.0 + scale) + shift


def _silu(x):
    return x * (1.0 / (1.0 + jnp.exp(-x)))


def _mod_kernel(c_ref, w_ref, b_ref, o_ref):
    s = _silu(c_ref[...]).astype(BF16)
    o_ref[...] = jnp.dot(s, w_ref[...].astype(BF16), preferred_element_type=F32) + b_ref[...]


def _mod_table(c_all, w_mod, b_mod):
    depth, d, n = w_mod.shape
    tn = _pick(n, (1024, 512, 256, 128))
    return pl.pallas_call(
        _mod_kernel,
        grid=(depth, n // tn),
        in_specs=[
            pl.BlockSpec((8, d), lambda l, j: (0, 0)),
            pl.BlockSpec((None, d, tn), lambda l, j: (l, 0, j)),
            pl.BlockSpec((None, 1, tn), lambda l, j: (l, 0, j)),
        ],
        out_specs=pl.BlockSpec((None, 8, tn), lambda l, j: (l, 0, j)),
        out_shape=jax.ShapeDtypeStruct((depth, 8, n), F32),
        compiler_params=_params(("parallel", "parallel")),
        name="mod_table",
    )(c_all, w_mod, b_mod.reshape(depth, 1, n))


def _in_proj_kernel(x_ref, g_ref, mod_ref, w_ref, cos_ref, sa_ref, sb_ref, o_ref, h_ref, *, tm, tn, d, geo):
    i = pl.program_id(0)
    j = pl.program_id(1)

    @pl.when(j == 0)
    def _():
        for u in range(tm // SUB):
            row = _mod_row(i * (tm // SUB) + u, geo)
            rs = slice(u * SUB, (u + 1) * SUB)
            shift = mod_ref[pl.ds(row, 1), 0:d]
            scale = mod_ref[pl.ds(row, 1), d:2 * d]
            h_ref[rs, :] = _normmod(x_ref[rs, :], g_ref[...], shift, scale).astype(BF16)

    acc = jnp.dot(h_ref[...], w_ref[...], preferred_element_type=F32)

    def rope(scale):
        outs = []
        for hs in range(tn // LANES):
            a = acc[:, hs * LANES:(hs + 1) * LANES]
            r = (a * cos_ref[...] + pltpu.roll(a, LANES - 32, 1) * sa_ref[...]
                 + pltpu.roll(a, 32, 1) * sb_ref[...])
            outs.append(r * scale)
        return jnp.concatenate(outs, axis=1)

    @pl.when(j == COL_Q // tn)
    def _():
        o_ref[...] = rope(ATT_QK_DIM ** -0.5 * math.log2(math.e)).astype(BF16)

    @pl.when(j == COL_K // tn)
    def _():
        o_ref[...] = rope(1.0).astype(BF16)

    @pl.when(j >= COL_V // tn)
    def _():
        o_ref[...] = acc.astype(BF16)


def _in_proj(x, g, mod, w, cos, sa, sb, geo):
    m, d = x.shape
    n = w.shape[1]
    tm = _pick(m, (1024, 512, 256))
    tn = ATT_WIDTH
    kern = functools.partial(_in_proj_kernel, tm=tm, tn=tn, d=d, geo=geo)
    return pl.pallas_call(
        kern,
        grid=(m // tm, n // tn),
        in_specs=[
            pl.BlockSpec((tm, d), lambda i, j: (i, 0)),
            pl.BlockSpec((1, d), lambda i, j: (0, 0)),
            pl.BlockSpec(mod.shape, lambda i, j: (0, 0)),
            pl.BlockSpec((d, tn), lambda i, j: (0, j)),
            pl.BlockSpec((tm, LANES), lambda i, j: (i, 0)),
            pl.BlockSpec((tm, LANES), lambda i, j: (i, 0)),
            pl.BlockSpec((tm, LANES), lambda i, j: (i, 0)),
        ],
        out_specs=pl.BlockSpec((tm, tn), lambda i, j: (i, j)),
        out_shape=jax.ShapeDtypeStruct((m, n), BF16),
        scratch_shapes=[pltpu.VMEM((tm, d), BF16)],
        compiler_params=_params(("parallel", "arbitrary")),
        name="in_proj",
    )(x, g.reshape(1, d), mod, w, cos, sa, sb)


def _attn_kernel(lam_ref, q_ref, k_ref, v_ref, g_ref, o_ref, s_ref, vt_ref, *, tq, tk, ctx_tiles, ctx_chunks,
                 all_chunks, out_scale):
    qi = pl.program_id(2)
    hp = q_ref.shape[1] // LANES
    heads = [slice(h * LANES, (h + 1) * LANES) for h in range(hp)]

    @pl.when(qi == 0)
    def _():
        for h, hs in enumerate(heads):
            for c in range(all_chunks):
                vt_ref[h, c] = v_ref[c * tk:(c + 1) * tk, hs].astype(F32).T.astype(BF16)

    row = lax.broadcasted_iota(jnp.int32, (LANES, tq), 0)
    q2ts = []
    for hs in heads:
        qt = q_ref[:, hs].astype(F32).T
        q2ts.append(jnp.concatenate([jnp.where(row < ATT_QK_DIM, qt, 0.0), jnp.where(row >= ATT_QK_DIM, qt, 0.0)],
                                    axis=1).astype(BF16))

    def fold(x):
        return x.reshape(tk // 8, 8, x.shape[-1])

    def scores(h, n_chunks):
        mm = None
        for c in range(n_chunks):
            s = jnp.dot(k_ref[c * tk:(c + 1) * tk, heads[h]], q2ts[h], preferred_element_type=F32)
            s_ref[h, c] = s
            sf = jnp.max(fold(s), axis=0)
            mm = sf if mm is None else jnp.maximum(mm, sf)
        return jnp.max(mm, axis=0, keepdims=True)

    def values(h, n_chunks, m):
        ps = None
        acc = None
        for c in range(n_chunks):
            p = jnp.exp2(s_ref[h, c] - m)
            pf = jnp.sum(fold(p), axis=0)
            ps = pf if ps is None else ps + pf
            pv = jnp.dot(vt_ref[h, c], p.astype(BF16), preferred_element_type=F32)
            acc = pv if acc is None else acc + pv
        ot = acc * (1.0 / jnp.sum(ps, axis=0, keepdims=True))
        o = (ot[:, :tq] - lam_ref[0] * ot[:, tq:]).T
        ms = jnp.mean(o * o, axis=-1, keepdims=True)
        o = (o * lax.rsqrt(ms + EPS)) * g_ref[...] * out_scale
        o_ref[:, heads[h]] = o.astype(BF16)

    def attend(n_chunks):
        ms = [scores(h, n_chunks) for h in range(hp)]
        for h in range(hp):
            values(h, n_chunks, ms[h])

    @pl.when(qi < ctx_tiles)
    def _():
        attend(ctx_chunks)

    @pl.when(qi >= ctx_tiles)
    def _():
        attend(all_chunks)


def _attention(p3, lam, subln_g, lambda_init, ctx_len):
    b, t, _ = p3.shape
    tq = SUB
    tk = SUB
    hp = ATT_HEADS_PER_STEP
    w = hp * LANES
    kern = functools.partial(_attn_kernel, tq=tq, tk=tk, ctx_tiles=ctx_len // tq, ctx_chunks=ctx_len // tk,
                             all_chunks=t // tk, out_scale=1.0 - lambda_init)
    return pl.pallas_call(
        kern,
        grid=(b, ATT_HEADS // hp, t // tq),
        in_specs=[
            pl.BlockSpec(memory_space=pltpu.SMEM),
            pl.BlockSpec((None, tq, w), lambda bi, h, qi: (bi, qi, COL_Q // w + h)),
            pl.BlockSpec((None, t, w), lambda bi, h, qi: (bi, 0, COL_K // w + h)),
            pl.BlockSpec((None, t, w), lambda bi, h, qi: (bi, 0, COL_V // w + h)),
            pl.BlockSpec((1, LANES), lambda bi, h, qi: (0, 0)),
        ],
        out_specs=pl.BlockSpec((None, tq, w), lambda bi, h, qi: (bi, qi, h)),
        out_shape=jax.ShapeDtypeStruct((b, t, ATT_WIDTH), BF16),
        scratch_shapes=[pltpu.VMEM((hp, t // tk, tk, 2 * tq), F32), pltpu.VMEM((hp, t // tk, LANES, tk), BF16)],
        compiler_params=_params(("parallel", "parallel", "arbitrary")),
        name="diff_attention",
    )(lam, p3, p3, p3, subln_g.reshape(1, LANES))


CONV_HALO = 16


def _conv_kernel(prev_ref, cur_ref, next_ref, w_ref, b_ref, o_ref, *, rb, ctx_blocks, n_blocks):
    i = pl.program_id(1)
    prev_ok = jnp.logical_and(i != 0, i != ctx_blocks)
    next_ok = jnp.logical_and(i != ctx_blocks - 1, i != n_blocks - 1)
    prev = jnp.where(prev_ok, prev_ref[...].astype(F32), 0.0)
    nxt = jnp.where(next_ok, next_ref[...].astype(F32), 0.0)
    xc = jnp.concatenate([prev, cur_ref[...].astype(F32), nxt], axis=0)
    pad = SSM_CONV // 2
    acc = b_ref[...] + w_ref[0:1, :] * xc[CONV_HALO - pad:CONV_HALO - pad + rb, :]
    for k in range(1, SSM_CONV):
        s = CONV_HALO - pad + k
        acc = acc + w_ref[k:k + 1, :] * xc[s:s + rb, :]
    o_ref[...] = _silu(acc).astype(BF16)


def _conv_silu(p3, conv_w, conv_b, ctx_len):
    b, t, _ = p3.shape
    rb = SUB
    tc = 512
    hb = rb // CONV_HALO
    n_blocks = t // rb
    c0 = COL_XBC // tc
    kern = functools.partial(_conv_kernel, rb=rb, ctx_blocks=ctx_len // rb, n_blocks=n_blocks)
    last_halo = t // CONV_HALO - 1
    return pl.pallas_call(
        kern,
        grid=(b, n_blocks, XBC_COLS // tc),
        in_specs=[
            pl.BlockSpec((None, CONV_HALO, tc), lambda bi, i, j: (bi, jnp.maximum(i * hb - 1, 0), c0 + j)),
            pl.BlockSpec((None, rb, tc), lambda bi, i, j: (bi, i, c0 + j)),
            pl.BlockSpec((None, CONV_HALO, tc), lambda bi, i, j: (bi, jnp.minimum((i + 1) * hb, last_halo), c0 + j)),
            pl.BlockSpec((SSM_CONV, tc), lambda bi, i, j: (0, j)),
            pl.BlockSpec((1, tc), lambda bi, i, j: (0, j)),
        ],
        out_specs=pl.BlockSpec((None, rb, tc), lambda bi, i, j: (bi, i, j)),
        out_shape=jax.ShapeDtypeStruct((b, t, XBC_COLS), BF16),
        compiler_params=_params(("parallel", "parallel", "parallel")),
        name="conv_silu",
    )(p3, p3, p3, conv_w, conv_b.reshape(1, XBC_COLS))


def _split_bf16(a, pieces):
    parts = []
    r = a
    for _ in range(pieces):
        p = r.astype(BF16)
        parts.append(p)
        r = r - p.astype(F32)
    return parts


def _dot_sel(a, sel, pieces):
    out = None
    for p in _split_bf16(a, pieces):
        t = jnp.dot(p, sel, preferred_element_type=F32)
        out = t if out is None else out + t
    return out


def _sel_dot(sel, a, pieces):
    out = None
    for p in _split_bf16(a, pieces):
        t = jnp.dot(sel, p, preferred_element_type=F32)
        out = t if out is None else out + t
    return out


def _ssd_chunk(xs, bm, cm, dtb, bias, alog, h_ref, tri, ex64, reverse):
    q = xs.shape[0]
    hp = SSM_HEADS_PER_GROUP
    dt = jax.nn.softplus(dtb.astype(F32) + bias)
    a = dt * (-jnp.exp(alog))
    acs = _sel_dot(tri, a, 3)
    acs_t = acs.T
    end = acs[0:1, :] if reverse else acs[q - 1:q, :]
    dt_e = _dot_sel(dt, ex64, 2)
    eacs_e = _dot_sel(jnp.exp(acs), ex64, 2)
    dte_e = _dot_sel(jnp.exp(end - acs), ex64, 2)
    xsf = xs.astype(F32)
    xdt = xsf * dt_e
    cb = lax.dot_general(cm, bm, (((1,), (1,)), ((), ())), preferred_element_type=F32)
    qi = lax.broadcasted_iota(jnp.int32, (q, q), 0)
    si = lax.broadcasted_iota(jnp.int32, (q, q), 1)
    mask = (si >= qi) if reverse else (si <= qi)
    lane = lax.broadcasted_iota(jnp.int32, (q, LANES), 1)
    ys = []
    for j in range(hp // 2):
        xp = xdt[:, j * LANES:(j + 1) * LANES]
        halves = (jnp.where(lane < SSM_HEAD_DIM, xp, 0.0).astype(BF16),
                  jnp.where(lane >= SSM_HEAD_DIM, xp, 0.0).astype(BF16))
        acc = None
        for k in range(2):
            r = 2 * j + k
            seg = jnp.broadcast_to(acs[:, r:r + 1], (q, LANES)) - acs_t[r:r + 1, :]
            dec = jnp.exp(jnp.where(mask, seg, -1e30))
            t = jnp.dot((cb * dec).astype(BF16), halves[k], preferred_element_type=F32)
            acc = t if acc is None else acc + t
        ys.append(acc)
    y_diag = jnp.concatenate(ys, axis=1)
    h = h_ref[...]
    y_off = jnp.dot(cm, h.astype(BF16), preferred_element_type=F32) * eacs_e
    cdec = eacs_e[0:1, :] if reverse else eacs_e[q - 1:q, :]
    xd = (xdt * dte_e).astype(BF16)
    h_ref[...] = h * cdec + lax.dot_general(bm, xd, (((0,), (0,)), ((), ())), preferred_element_type=F32)
    return y_diag + y_off, xsf


def _ssd_kernel(xsf_ref, bmf_ref, cmf_ref, dtf_ref, xsb_ref, bmb_ref, cmb_ref, dtb_ref,
                biasf_ref, biasb_ref, alogf_ref, alogb_ref, dskip_ref, tril_ref, triu_ref, ex64_ref,
                yf_ref, yb_ref, hf_ref, hb_ref):
    @pl.when(pl.program_id(2) == 0)
    def _():
        hf_ref[...] = jnp.zeros_like(hf_ref)
        hb_ref[...] = jnp.zeros_like(hb_ref)

    yf, xsf = _ssd_chunk(xsf_ref[...], bmf_ref[...], cmf_ref[...], dtf_ref[...], biasf_ref[...], alogf_ref[...],
                         hf_ref, tril_ref[...], ex64_ref[...], False)
    yf_ref[...] = yf + dskip_ref[...] * xsf
    yb, _ = _ssd_chunk(xsb_ref[...], bmb_ref[...], cmb_ref[...], dtb_ref[...], biasb_ref[...], alogb_ref[...],
                       hb_ref, triu_ref[...], ex64_ref[...], True)
    yb_ref[...] = yb


def _ssd_scan(u3, p3, bias4, alog4, dskip, ctx_len):
    b, t, _ = u3.shape
    q = SSM_CHUNK
    ns = t // q
    ncc = ctx_len // q
    g_n = SSM_GROUPS

    def fwd(s):
        return s

    def bwd(s):
        return jnp.where(s < ncc, ncc - 1 - s, ns - 1 + ncc - s)

    xs_blk = GROUP_WIDTH // LANES
    b_col = SSM_WIDTH // LANES
    c_col = b_col + g_n
    dt_col = COL_DT // LANES
    del xs_blk

    def specs(cidx, direction):
        return [
            pl.BlockSpec((None, q, GROUP_WIDTH), lambda bi, g, s: (bi, cidx(s), g)),
            pl.BlockSpec((None, q, LANES), lambda bi, g, s: (bi, cidx(s), b_col + g)),
            pl.BlockSpec((None, q, LANES), lambda bi, g, s: (bi, cidx(s), c_col + g)),
            pl.BlockSpec((None, q, LANES), lambda bi, g, s: (bi, cidx(s), dt_col + direction * g_n + g)),
        ]

    row = lambda direction: pl.BlockSpec((None, 1, LANES), lambda bi, g, s: (direction * g_n + g, 0, 0))
    full = lambda arr: pl.BlockSpec(arr.shape, lambda bi, g, s: (0,) * arr.ndim)

    ri = lax.broadcasted_iota(jnp.int32, (q, q), 0)
    ci = lax.broadcasted_iota(jnp.int32, (q, q), 1)
    tril = (ci <= ri).astype(BF16)
    triu = (ci >= ri).astype(BF16)
    r64 = lax.broadcasted_iota(jnp.int32, (LANES, GROUP_WIDTH), 0)
    c64 = lax.broadcasted_iota(jnp.int32, (LANES, GROUP_WIDTH), 1)
    ex64 = (c64 // SSM_HEAD_DIM == r64).astype(BF16)

    in_specs = (specs(fwd, 0) + specs(bwd, 1) + [row(0), row(1), row(0), row(1),
                pl.BlockSpec((None, 1, GROUP_WIDTH), lambda bi, g, s: (g, 0, 0)),
                full(tril), full(triu), full(ex64)])
    out_spec = lambda cidx: pl.BlockSpec((None, q, GROUP_WIDTH), lambda bi, g, s: (bi, cidx(s), g))
    return pl.pallas_call(
        _ssd_kernel,
        grid=(b, g_n, ns),
        in_specs=in_specs,
        out_specs=[out_spec(fwd), out_spec(bwd)],
        out_shape=[jax.ShapeDtypeStruct((b, t, SSM_WIDTH), F32)] * 2,
        scratch_shapes=[pltpu.VMEM((SSM_STATE, GROUP_WIDTH), F32)] * 2,
        compiler_params=_params(("parallel", "parallel", "arbitrary")),
        name="ssd_scan",
    )(u3, u3, u3, p3, u3, u3, u3, p3, bias4, bias4, alog4, alog4, dskip, tril, triu, ex64)


def _ssd_finish_kernel(yf_ref, yb_ref, z_ref, g_ref, o_ref):
    y = (yf_ref[...] + yb_ref[...]) * _silu(z_ref[...].astype(F32))
    outs = []
    for g in range(SSM_GROUPS):
        yg = y[:, g * GROUP_WIDTH:(g + 1) * GROUP_WIDTH]
        ms = jnp.mean(yg * yg, axis=-1, keepdims=True)
        outs.append((yg * lax.rsqrt(ms + EPS)) * g_ref[:, g * GROUP_WIDTH:(g + 1) * GROUP_WIDTH])
    o_ref[...] = jnp.concatenate(outs, axis=1).astype(BF16)


def _ssd_finish(yf, yb, p, norm_g):
    m = yf.shape[0]
    tm = _pick(m, (512, 256))
    return pl.pallas_call(
        _ssd_finish_kernel,
        grid=(m // tm,),
        in_specs=[
            pl.BlockSpec((tm, SSM_WIDTH), lambda i: (i, 0)),
            pl.BlockSpec((tm, SSM_WIDTH), lambda i: (i, 0)),
            pl.BlockSpec((tm, SSM_WIDTH), lambda i: (i, COL_Z // SSM_WIDTH)),
            pl.BlockSpec((1, SSM_WIDTH), lambda i: (0, 0)),
        ],
        out_specs=pl.BlockSpec((tm, SSM_WIDTH), lambda i: (i, 0)),
        out_shape=jax.ShapeDtypeStruct((m, SSM_WIDTH), BF16),
        compiler_params=_params(("parallel",)),
        name="ssd_finish",
    )(yf, yb, p, norm_g.reshape(1, SSM_WIDTH))


def _mm_resid_kernel(*refs, n_a, tm, geo):
    a_refs = refs[:n_a]
    w_refs = refs[n_a:2 * n_a]
    x_ref, gate_ref, o_ref = refs[2 * n_a:]
    i = pl.program_id(0)
    acc = None
    for a_ref, w_ref in zip(a_refs, w_refs):
        t = jnp.dot(a_ref[...], w_ref[...], preferred_element_type=F32)
        acc = t if acc is None else acc + t
    for u in range(tm // SUB):
        row = _mod_row(i * (tm // SUB) + u, geo)
        rs = slice(u * SUB, (u + 1) * SUB)
        o_ref[rs, :] = x_ref[rs, :] + gate_ref[pl.ds(row, 1), :] * acc[rs, :]


def _mm_resid(a_list, w_list, x, mod, gate_idx, geo, tn):
    m, d = x.shape
    tm = _pick(m, (1024, 512, 256))
    n_a = len(a_list)
    kern = functools.partial(_mm_resid_kernel, n_a=n_a, tm=tm, geo=geo)
    gate_blk = gate_idx * d // tn
    in_specs = ([pl.BlockSpec((tm, a.shape[1]), lambda i, j: (i, 0)) for a in a_list]
                + [pl.BlockSpec((w.shape[0], tn), lambda i, j: (0, j)) for w in w_list]
                + [pl.BlockSpec((tm, tn), lambda i, j: (i, j)),
                   pl.BlockSpec((mod.shape[0], tn), lambda i, j: (0, gate_blk + j))])
    return pl.pallas_call(
        kern,
        grid=(m // tm, d // tn),
        in_specs=in_specs,
        out_specs=pl.BlockSpec((tm, tn), lambda i, j: (i, j)),
        out_shape=jax.ShapeDtypeStruct((m, d), F32),
        compiler_params=_params(("parallel", "parallel")),
        name="matmul_gated_residual",
    )(*a_list, *w_list, x, mod)


def _ffn_in_kernel(x_ref, g_ref, mod_ref, w1_ref, w3_ref, o_ref, h_ref, *, tm, d, geo):
    i = pl.program_id(0)

    @pl.when(pl.program_id(1) == 0)
    def _():
        for u in range(tm // SUB):
            row = _mod_row(i * (tm // SUB) + u, geo)
            rs = slice(u * SUB, (u + 1) * SUB)
            shift = mod_ref[pl.ds(row, 1), 3 * d:4 * d]
            scale = mod_ref[pl.ds(row, 1), 4 * d:5 * d]
            h_ref[rs, :] = _normmod(x_ref[rs, :], g_ref[...], shift, scale).astype(BF16)

    h = h_ref[...]
    a = jnp.dot(h, w1_ref[...], preferred_element_type=F32)
    b = jnp.dot(h, w3_ref[...], preferred_element_type=F32)
    o_ref[...] = (_silu(a) * b).astype(BF16)


def _ffn_in(x, g, mod, w1, w3, geo):
    m, d = x.shape
    ff = w1.shape[1]
    tm = _pick(m, (1024, 512, 256))
    tf = _pick(ff, (512, 256, 128))
    kern = functools.partial(_ffn_in_kernel, tm=tm, d=d, geo=geo)
    return pl.pallas_call(
        kern,
        grid=(m // tm, ff // tf),
        in_specs=[
            pl.BlockSpec((tm, d), lambda i, j: (i, 0)),
            pl.BlockSpec((1, d), lambda i, j: (0, 0)),
            pl.BlockSpec(mod.shape, lambda i, j: (0, 0)),
            pl.BlockSpec((d, tf), lambda i, j: (0, j)),
            pl.BlockSpec((d, tf), lambda i, j: (0, j)),
        ],
        out_specs=pl.BlockSpec((tm, tf), lambda i, j: (i, j)),
        out_shape=jax.ShapeDtypeStruct((m, ff), BF16),
        scratch_shapes=[pltpu.VMEM((tm, d), BF16)],
        compiler_params=_params(("parallel", "arbitrary")),
        name="ffn_in",
    )(x, g.reshape(1, d), mod, w1, w3)


def _store_row_tiles(dst_ref, src_ref, rt):
    tm = src_ref.shape[0]
    for a in range(rt):
        dst_ref[pl.ds(a, tm, stride=rt), :] = src_ref[:, a * LANES:(a + 1) * LANES]


def _load_row_tiles(src_ref, row0, tm, rt):
    return jnp.concatenate([src_ref[pl.ds(row0 * rt + a, tm, stride=rt), :] for a in range(rt)], axis=1)


def _router_kernel(x_ref, g_ref, mod_ref, rw_ref, h_ref, r_ref, hs_ref, *, tm, d, n_exp, geo):
    i = pl.program_id(0)
    for u in range(tm // SUB):
        row = _mod_row(i * (tm // SUB) + u, geo)
        rs = slice(u * SUB, (u + 1) * SUB)
        shift = mod_ref[pl.ds(row, 1), 3 * d:4 * d]
        scale = mod_ref[pl.ds(row, 1), 4 * d:5 * d]
        hs_ref[rs, :] = _normmod(x_ref[rs, :], g_ref[...], shift, scale)
    _store_row_tiles(h_ref, hs_ref, d // LANES)
    logits = jnp.dot(hs_ref[...].astype(BF16), rw_ref[...], preferred_element_type=F32)
    lane = lax.broadcasted_iota(jnp.int32, logits.shape, 1)
    neg = jnp.float32(-1e30)
    logits = jnp.where(lane < n_exp, logits, neg)
    m1 = jnp.max(logits, axis=-1, keepdims=True)
    i1 = jnp.min(jnp.where(logits == m1, lane, LANES), axis=-1, keepdims=True)
    rest = jnp.where(lane == i1, neg, logits)
    m2 = jnp.max(rest, axis=-1, keepdims=True)
    i2 = jnp.min(jnp.where(rest == m2, lane, LANES), axis=-1, keepdims=True)
    t = jnp.exp(m2 - m1)
    g1 = 1.0 / (1.0 + t)
    g2 = t * g1
    out = jnp.where(lane == 0, i1.astype(F32),
                    jnp.where(lane == 1, i2.astype(F32),
                              jnp.where(lane == 2, g1, jnp.where(lane == 3, g2, 0.0))))
    r_ref[...] = out


def _router(x, g, mod, rw_pad, n_exp, geo):
    m, d = x.shape
    rt = d // LANES
    tm = _pick(m, (512, 256))
    kern = functools.partial(_router_kernel, tm=tm, d=d, n_exp=n_exp, geo=geo)
    return pl.pallas_call(
        kern,
        grid=(m // tm,),
        in_specs=[
            pl.BlockSpec((tm, d), lambda i: (i, 0)),
            pl.BlockSpec((1, d), lambda i: (0, 0)),
            pl.BlockSpec(mod.shape, lambda i: (0, 0)),
            pl.BlockSpec((d, LANES), lambda i: (0, 0)),
        ],
        out_specs=[pl.BlockSpec((tm * rt, LANES), lambda i: (i, 0)), pl.BlockSpec((tm, LANES), lambda i: (i, 0))],
        out_shape=[jax.ShapeDtypeStruct((m * rt, LANES), F32), jax.ShapeDtypeStruct((m, LANES), F32)],
        scratch_shapes=[pltpu.VMEM((tm, d), F32)],
        compiler_params=_params(("parallel",)),
        name="moe_router",
    )(x, g.reshape(1, d), mod, rw_pad)


GATHER_ROWS = 512


def _gather_kernel(*refs, n_out, rows, rt):
    idx_refs = refs[:n_out]
    src_ref = refs[n_out]
    dst_refs = refs[n_out + 1:2 * n_out + 1]
    sem = refs[2 * n_out + 1]

    def copy(k, r):
        src_row = pl.multiple_of(idx_refs[k][r] * rt, rt)
        dst_row = pl.multiple_of(r * rt, rt)
        return pltpu.make_async_copy(src_ref.at[pl.ds(src_row, rt), :], dst_refs[k].at[pl.ds(dst_row, rt), :], sem)

    def start(r, c):
        for k in range(n_out):
            copy(k, r).start()
        return c

    def wait(r, c):
        for k in range(n_out):
            copy(k, r).wait()
        return c

    lax.fori_loop(0, rows, start, 0)
    lax.fori_loop(0, rows, wait, 0)


def _gather_rows(src, idx_list, rt):
    n = idx_list[0].shape[0]
    rows = GATHER_ROWS
    assert n % rows == 0
    n_out = len(idx_list)
    kern = functools.partial(_gather_kernel, n_out=n_out, rows=rows, rt=rt)
    outs = pl.pallas_call(
        kern,
        grid=(n // rows,),
        in_specs=([pl.BlockSpec((rows,), lambda i: (i,), memory_space=pltpu.SMEM) for _ in idx_list]
                  + [pl.BlockSpec(memory_space=pl.ANY)]),
        out_specs=[pl.BlockSpec((rows * rt, LANES), lambda i: (i, 0)) for _ in idx_list],
        out_shape=[jax.ShapeDtypeStruct((n * rt, LANES), src.dtype) for _ in idx_list],
        scratch_shapes=[pltpu.SemaphoreType.DMA(())],
        compiler_params=_params(("arbitrary",), has_side_effects=True),
        name="gather_rows",
    )(*idx_list, src)
    return outs


def _moe_in_kernel(te_ref, nu_ref, h_ref, w1_ref, w3_ref, o_ref, hb_ref):
    t = pl.program_id(0)
    j = pl.program_id(1)
    tm, d = hb_ref.shape

    @pl.when(t < nu_ref[0])
    def _():
        @pl.when(j == 0)
        def _():
            hb_ref[...] = _load_row_tiles(h_ref, 0, tm, d // LANES).astype(BF16)

        h = hb_ref[...]
        a = jnp.dot(h, w1_ref[...], preferred_element_type=F32)
        b = jnp.dot(h, w3_ref[...], preferred_element_type=F32)
        o_ref[...] = (_silu(a) * b).astype(BF16)

    @pl.when(t >= nu_ref[0])
    def _():
        o_ref[...] = jnp.zeros_like(o_ref)


def _moe_in(hs, w1, w3, layer, tile_expert, n_used, tm):
    d = w1.shape[2]
    rt = d // LANES
    s = hs.shape[0] // rt
    ff = w1.shape[3]
    tf = _pick(ff, (1408, 512, 256, 128))
    n_ff = ff // tf

    def w_map(t, j, te, nu):
        return (layer, te[t], 0, jnp.where(t < nu[0], j, n_ff - 1))

    return pl.pallas_call(
        _moe_in_kernel,
        grid_spec=pltpu.PrefetchScalarGridSpec(
            num_scalar_prefetch=2,
            grid=(s // tm, n_ff),
            in_specs=[
                pl.BlockSpec((tm * rt, LANES), lambda t, j, te, nu: (t, 0)),
                pl.BlockSpec((None, None, d, tf), w_map),
                pl.BlockSpec((None, None, d, tf), w_map),
            ],
            out_specs=pl.BlockSpec((tm, tf), lambda t, j, te, nu: (t, j)),
            scratch_shapes=[pltpu.VMEM((tm, d), BF16)],
        ),
        out_shape=jax.ShapeDtypeStruct((s, ff), BF16),
        compiler_params=_params(("arbitrary", "arbitrary")),
        name="moe_experts_in",
    )(tile_expert, n_used, hs, w1, w3)


def _moe_out_kernel(te_ref, nu_ref, u_ref, w2_ref, gate_ref, o_ref, *, rt):
    t = pl.program_id(0)
    j = pl.program_id(1)
    tm = u_ref.shape[0]
    cols = w2_ref.shape[1] // LANES
    valid = t < nu_ref[0]

    @pl.when(valid)
    def _():
        y = gate_ref[...] * jnp.dot(u_ref[...], w2_ref[...], preferred_element_type=F32)
        for a in range(cols):
            o_ref[pl.ds(j * cols + a, tm, stride=rt), :] = y[:, a * LANES:(a + 1) * LANES]

    @pl.when(jnp.logical_and(jnp.logical_not(valid), j == 0))
    def _():
        o_ref[...] = jnp.zeros_like(o_ref)


def _moe_out(u, w2, layer, slot_gate, tile_expert, n_used, tm):
    s, ff = u.shape
    d = w2.shape[3]
    rt = d // LANES
    tn = _pick(d, (512, 256, 128))
    n_n = d // tn

    def w_map(t, j, te, nu):
        return (layer, te[t], 0, jnp.where(t < nu[0], j, n_n - 1))

    return pl.pallas_call(
        functools.partial(_moe_out_kernel, rt=rt),
        grid_spec=pltpu.PrefetchScalarGridSpec(
            num_scalar_prefetch=2,
            grid=(s // tm, n_n),
            in_specs=[
                pl.BlockSpec((tm, ff), lambda t, j, te, nu: (t, 0)),
                pl.BlockSpec((None, None, ff, tn), w_map),
                pl.BlockSpec((tm, 1), lambda t, j, te, nu: (t, 0)),
            ],
            out_specs=pl.BlockSpec((tm * rt, LANES), lambda t, j, te, nu: (t, 0)),
        ),
        out_shape=jax.ShapeDtypeStruct((s * rt, LANES), F32),
        compiler_params=_params(("arbitrary", "arbitrary")),
        name="moe_experts_out",
    )(tile_expert, n_used, u, w2, slot_gate)


def _moe_combine_kernel(x_ref, ya_ref, yb_ref, gate_ref, o_ref, *, tm, geo):
    i = pl.program_id(0)
    rt = x_ref.shape[1] // LANES
    for u in range(tm // SUB):
        row = _mod_row(i * (tm // SUB) + u, geo)
        rs = slice(u * SUB, (u + 1) * SUB)
        y = _load_row_tiles(ya_ref, u * SUB, SUB, rt) + _load_row_tiles(yb_ref, u * SUB, SUB, rt)
        o_ref[rs, :] = x_ref[rs, :] + gate_ref[pl.ds(row, 1), :] * y


def _moe_combine(x, ya, yb, mod, geo):
    m, d = x.shape
    rt = d // LANES
    tm = _pick(m, (512, 256))
    kern = functools.partial(_moe_combine_kernel, tm=tm, geo=geo)
    blk = pl.BlockSpec((tm, d), lambda i: (i, 0))
    blk3 = pl.BlockSpec((tm * rt, LANES), lambda i: (i, 0))
    return pl.pallas_call(
        kern,
        grid=(m // tm,),
        in_specs=[blk, blk3, blk3, pl.BlockSpec((mod.shape[0], d), lambda i: (0, 5))],
        out_specs=blk,
        out_shape=jax.ShapeDtypeStruct((m, d), F32),
        compiler_params=_params(("parallel",)),
        name="moe_combine",
    )(x, ya, yb, mod)


def _moe_plan(route, n_exp, tm, n_tiles):
    m = route.shape[0]
    e = route[:, 0:TOP_K].astype(jnp.int32).reshape(-1)
    gate = route[:, TOP_K:2 * TOP_K].reshape(-1)
    onehot = (e[:, None] == jnp.arange(n_exp, dtype=jnp.int32)[None, :]).astype(jnp.int32)
    csum = jnp.cumsum(onehot, axis=0)
    rank = jnp.sum(csum * onehot, axis=1) - 1
    counts = csum[-1]
    tiles_per = (counts + tm - 1) // tm
    tile_end = jnp.cumsum(tiles_per)
    tile_start = tile_end - tiles_per
    pos = tile_start[e] * tm + rank
    n_used = tile_end[-1]
    n_slots = n_tiles * tm
    token = jnp.arange(TOP_K * m, dtype=jnp.int32) // TOP_K
    slot_token = jnp.zeros((n_slots,), jnp.int32).at[pos].set(token)
    slot_gate = jnp.zeros((n_slots,), F32).at[pos].set(gate)
    tile_ids = jnp.arange(n_tiles, dtype=jnp.int32)
    tile_expert = jnp.sum((tile_ids[:, None] >= tile_end[None, :]).astype(jnp.int32), axis=1)
    last_expert = jnp.sum((n_used - 1 >= tile_end).astype(jnp.int32))
    tile_expert = jnp.where(tile_ids < n_used, tile_expert, last_expert).astype(jnp.int32)
    return pos.astype(jnp.int32), slot_token, slot_gate, tile_expert, n_used.reshape(1).astype(jnp.int32)


def _moe_layer(x, g, mod, rw_pad, w1, w3, w2, layer, geo):
    m, d = x.shape
    rt = d // LANES
    n_exp = w1.shape[1]
    tm = 512
    n_tiles = -(-(TOP_K * m) // tm) + n_exp
    n_tiles = -(-(n_tiles * tm) // GATHER_ROWS) * GATHER_ROWS // tm
    h, route = _router(x, g, mod, rw_pad, n_exp, geo)
    pos, slot_token, slot_gate, tile_expert, n_used = _moe_plan(route, n_exp, tm, n_tiles)
    (hs,) = _gather_rows(h, [slot_token], rt)
    u = _moe_in(hs, w1, w3, layer, tile_expert, n_used, tm)
    y = _moe_out(u, w2, layer, slot_gate.reshape(-1, 1), tile_expert, n_used, tm)
    mp = -(-m // GATHER_ROWS) * GATHER_ROWS
    pos2 = jnp.pad(pos.reshape(m, TOP_K), ((0, mp - m), (0, 0)))
    ya, yb = _gather_rows(y, [pos2[:, 0], pos2[:, 1]], rt)
    return _moe_combine(x, ya[:m * rt], yb[:m * rt], mod, geo)


def _final_norm_kernel(x_ref, g_ref, o_ref):
    x = x_ref[...]
    ms = jnp.mean(x * x, axis=-1, keepdims=True)
    o_ref[...] = (x * lax.rsqrt(ms + EPS)) * g_ref[...]


def _final_norm(x3, g, ctx_len):
    b, t, d = x3.shape
    seq = t - ctx_len
    off = ctx_len // SUB
    return pl.pallas_call(
        _final_norm_kernel,
        grid=(b, seq // SUB),
        in_specs=[pl.BlockSpec((None, SUB, d), lambda bi, i: (bi, off + i, 0)),
                  pl.BlockSpec((1, d), lambda bi, i: (0, 0))],
        out_specs=pl.BlockSpec((None, SUB, d), lambda bi, i: (bi, i, 0)),
        out_shape=jax.ShapeDtypeStruct((b, seq, d), F32),
        compiler_params=_params(("parallel", "parallel")),
        name="final_norm",
    )(x3, g.reshape(1, d))


def _rope_tables(batch, seq, ctx_len):
    rows = seq // GRID_W
    row_pos = jnp.repeat(jnp.arange(rows, dtype=F32), GRID_W)
    col_pos = jnp.tile(jnp.arange(GRID_W, dtype=F32), rows)
    inv_freq = ROPE_THETA ** (-jnp.arange(ROPE_PAIRS_PER_AXIS, dtype=F32) / ROPE_PAIRS_PER_AXIS)
    ang = jnp.concatenate([row_pos[:, None] * inv_freq, col_pos[:, None] * inv_freq], axis=-1)
    cos = jnp.cos(ang)
    sin = jnp.sin(ang)
    zero = jnp.zeros_like(sin)
    cos128 = jnp.tile(cos, (1, 4))
    sa128 = jnp.tile(jnp.concatenate([-sin, zero], axis=-1), (1, 2))
    sb128 = jnp.tile(jnp.concatenate([zero, sin], axis=-1), (1, 2))

    def stream(tab, ctx_val):
        ctx_rows = jnp.full((ctx_len, LANES), ctx_val, F32)
        return jnp.tile(jnp.concatenate([ctx_rows, tab], axis=0), (batch, 1))

    return stream(cos128, 1.0), stream(sa128, 0.0), stream(sb128, 0.0)


def _pad_w_in(w_in):
    depth, d, _ = w_in.shape
    main = w_in[:, :, :COL_DT]
    dt = w_in[:, :, COL_DT:COL_DT + 2 * SSM_HEADS].reshape(depth, d, 2 * SSM_GROUPS, SSM_HEADS_PER_GROUP)
    dt = jnp.pad(dt, ((0, 0), (0, 0), (0, 0), (0, LANES - SSM_HEADS_PER_GROUP)))
    return jnp.concatenate([main, dt.reshape(depth, d, 2 * SSM_GROUPS * LANES)], axis=-1).astype(BF16)


def _lane_rows(v):
    depth = v.shape[0]
    v = v.reshape(depth, 2 * SSM_GROUPS, 1, SSM_HEADS_PER_GROUP)
    return jnp.pad(v, ((0, 0), (0, 0), (0, 0), (0, LANES - SSM_HEADS_PER_GROUP)))


def kernel(x, c, ctx, c_ctx, w_mod, b_mod, norm_mix_g, w_in, lambda_q1, lambda_k1, lambda_q2, lambda_k2, subln_g, conv_w, conv_b, a_log, dt_bias, d_skip, ssm_norm_g, w_out, norm_ffn_g, ffn_w1, ffn_w3, ffn_w2, router_w, moe_w1, moe_w3, moe_w2, final_g):
    batch, seq, d = x.shape
    ctx_len = ctx.shape[1]
    depth = w_mod.shape[0]
    t = ctx_len + seq
    m = batch * t
    assert ctx_len % SUB == 0 and seq % SUB == 0 and seq % GRID_W == 0 and batch < 8
    geo = (t // SUB, ctx_len // SUB, batch)

    w_in_p = _pad_w_in(w_in)
    w_out_b = w_out.astype(BF16)
    moe_w1_b, moe_w3_b, moe_w2_b = moe_w1.astype(BF16), moe_w3.astype(BF16), moe_w2.astype(BF16)
    n_exp = router_w.shape[-1]
    rw_pad = jnp.pad(router_w, ((0, 0), (0, 0), (0, LANES - n_exp))).astype(BF16)

    c_all = jnp.zeros((8, d), F32).at[:batch].set(c).at[batch].set(c_ctx)
    mod_all = _mod_table(c_all, w_mod, b_mod)
    cos, sa, sb = _rope_tables(batch, seq, ctx_len)
    bias_rows = _lane_rows(dt_bias)
    alog_rows = _lane_rows(a_log)
    dskip_rows = jnp.repeat(d_skip, SSM_HEAD_DIM, axis=-1).reshape(depth, SSM_GROUPS, 1, GROUP_WIDTH)

    xs = jnp.concatenate([ctx, x], axis=1).reshape(m, d)
    for li in range(depth):
        lambda_init = 0.8 - 0.6 * math.exp(-0.3 * li)
        mod = mod_all[li]
        lam = (jnp.exp(jnp.sum(lambda_q1[li] * lambda_k1[li])) - jnp.exp(jnp.sum(lambda_q2[li] * lambda_k2[li]))
               + lambda_init).reshape(1).astype(F32)

        p = _in_proj(xs, norm_mix_g[li], mod, w_in_p[li], cos, sa, sb, geo)
        p3 = p.reshape(batch, t, IN_COLS_PAD)
        att = _attention(p3, lam, subln_g[li], lambda_init, ctx_len)
        u3 = _conv_silu(p3, conv_w[li], conv_b[li], ctx_len)
        yf, yb = _ssd_scan(u3, p3, bias_rows[li], alog_rows[li], dskip_rows[li], ctx_len)
        ssm = _ssd_finish(yf.reshape(m, SSM_WIDTH), yb.reshape(m, SSM_WIDTH), p, ssm_norm_g[li])
        xs = _mm_resid([att.reshape(m, ATT_WIDTH), ssm], [w_out_b[li, :ATT_WIDTH], w_out_b[li, ATT_WIDTH:]],
                       xs, mod, 2, geo, tn=512)

        j = li // 2
        if li % 2 == 0:
            uu = _ffn_in(xs, norm_ffn_g[li], mod, ffn_w1[j].astype(BF16), ffn_w3[j].astype(BF16), geo)
            xs = _mm_resid([uu], [ffn_w2[j].astype(BF16)], xs, mod, 5, geo, tn=256)
        else:
            xs = _moe_layer(xs, norm_ffn_g[li], mod, rw_pad[j], moe_w1_b, moe_w3_b, moe_w2_b, j, geo)
    return _final_norm(xs.reshape(batch, t, d), final_g, ctx_len)
```

```python
import functools
import math

import jax
import jax.numpy as jnp
from jax import lax
from jax.experimental import pallas as pl
from jax.experimental.pallas import tpu as pltpu

F32 = jnp.float32
BF16 = jnp.bfloat16

GRID_W = 64
EPS = 1e-6
N_MOD = 6
ATT_HEADS = 8
ATT_QK_DIM = 64
ATT_V_DIM = 128
ATT_WIDTH = ATT_HEADS * ATT_V_DIM
ROPE_THETA = 10000.0
ROPE_PAIRS_PER_AXIS = ATT_QK_DIM // 4
SSM_HEAD_DIM = 64
SSM_GROUPS = 2
SSM_HEADS_PER_GROUP = 8
SSM_HEADS = SSM_GROUPS * SSM_HEADS_PER_GROUP
SSM_WIDTH = SSM_HEADS * SSM_HEAD_DIM
SSM_STATE = 128
SSM_CONV = 5
SSM_CHUNK = 128
GROUP_WIDTH = SSM_WIDTH // SSM_GROUPS
XBC_COLS = SSM_WIDTH + 2 * SSM_GROUPS * SSM_STATE
TOP_K = 2

LANES = 128
SUB = 256
ATT_HEADS_PER_STEP = 2
VMEM_LIMIT = 52 * 1024 * 1024

COL_Q = 0
COL_K = COL_Q + ATT_WIDTH
COL_V = COL_K + ATT_WIDTH
COL_Z = COL_V + ATT_WIDTH
COL_XBC = COL_Z + SSM_WIDTH
COL_DT = COL_XBC + XBC_COLS
IN_COLS_PAD = COL_DT + 2 * SSM_GROUPS * LANES


def _pick(n, cands):
    for c in cands:
        if n % c == 0:
            return c
    raise ValueError(f"no tile in {cands} divides {n}")


def _params(sem, vmem=VMEM_LIMIT, **kw):
    return pltpu.CompilerParams(dimension_semantics=sem, vmem_limit_bytes=vmem, **kw)


def _mod_row(sub_idx, geo):
    subs_per_batch, ctx_subs, n_batch = geo
    b = sub_idx // subs_per_batch
    w = sub_idx % subs_per_batch
    return jnp.where(w < ctx_subs, n_batch, b)


def _normmod(x, g, shift, scale):
    ms = jnp.mean(x * x, axis=-1, keepdims=True)
    hn = (x * lax.rsqrt(ms + EPS)) * g
    return hn * (1.0 + scale) + shift


def _silu(x):
    return x * (1.0 / (1.0 + jnp.exp(-x)))


def _mod_kernel(c_ref, w_ref, b_ref, o_ref):
    s = _silu(c_ref[...]).astype(BF16)
    o_ref[...] = jnp.dot(s, w_ref[...].astype(BF16), preferred_element_type=F32) + b_ref[...]


def _mod_table(c_all, w_mod, b_mod):
    depth, d, n = w_mod.shape
    tn = _pick(n, (1024, 512, 256, 128))
    return pl.pallas_call(
        _mod_kernel,
        grid=(depth, n // tn),
        in_specs=[
            pl.BlockSpec((8, d), lambda l, j: (0, 0)),
            pl.BlockSpec((None, d, tn), lambda l, j: (l, 0, j)),
            pl.BlockSpec((None, 1, tn), lambda l, j: (l, 0, j)),
        ],
        out_specs=pl.BlockSpec((None, 8, tn), lambda l, j: (l, 0, j)),
        out_shape=jax.ShapeDtypeStruct((depth, 8, n), F32),
        compiler_params=_params(("parallel", "parallel")),
        name="mod_table",
    )(c_all, w_mod, b_mod.reshape(depth, 1, n))


def _in_proj_kernel(x_ref, g_ref, mod_ref, w_ref, cos_ref, sa_ref, sb_ref, o_ref, h_ref, *, tm, tn, d, geo):
    i = pl.program_id(0)
    j = pl.program_id(1)

    @pl.when(j == 0)
    def _():
        for u in range(tm // SUB):
            row = _mod_row(i * (tm // SUB) + u, geo)
            rs = slice(u * SUB, (u + 1) * SUB)
            shift = mod_ref[pl.ds(row, 1), 0:d]
            scale = mod_ref[pl.ds(row, 1), d:2 * d]
            h_ref[rs, :] = _normmod(x_ref[rs, :], g_ref[...], shift, scale).astype(BF16)

    acc = jnp.dot(h_ref[...], w_ref[...], preferred_element_type=F32)

    def rope(scale):
        outs = []
        for hs in range(tn // LANES):
            a = acc[:, hs * LANES:(hs + 1) * LANES]
            r = (a * cos_ref[...] + pltpu.roll(a, LANES - 32, 1) * sa_ref[...]
                 + pltpu.roll(a, 32, 1) * sb_ref[...])
            outs.append(r * scale)
        return jnp.concatenate(outs, axis=1)

    @pl.when(j == COL_Q // tn)
    def _():
        o_ref[...] = rope(ATT_QK_DIM ** -0.5 * math.log2(math.e)).astype(BF16)

    @pl.when(j == COL_K // tn)
    def _():
        o_ref[...] = rope(1.0).astype(BF16)

    @pl.when(j >= COL_V // tn)
    def _():
        o_ref[...] = acc.astype(BF16)


def _in_proj(x, g, mod, w, cos, sa, sb, geo):
    m, d = x.shape
    n = w.shape[1]
    tm = _pick(m, (1024, 512, 256))
    tn = ATT_WIDTH
    kern = functools.partial(_in_proj_kernel, tm=tm, tn=tn, d=d, geo=geo)
    return pl.pallas_call(
        kern,
        grid=(m // tm, n // tn),
        in_specs=[
            pl.BlockSpec((tm, d), lambda i, j: (i, 0)),
            pl.BlockSpec((1, d), lambda i, j: (0, 0)),
            pl.BlockSpec(mod.shape, lambda i, j: (0, 0)),
            pl.BlockSpec((d, tn), lambda i, j: (0, j)),
            pl.BlockSpec((tm, LANES), lambda i, j: (i, 0)),
            pl.BlockSpec((tm, LANES), lambda i, j: (i, 0)),
            pl.BlockSpec((tm, LANES), lambda i, j: (i, 0)),
        ],
        out_specs=pl.BlockSpec((tm, tn), lambda i, j: (i, j)),
        out_shape=jax.ShapeDtypeStruct((m, n), BF16),
        scratch_shapes=[pltpu.VMEM((tm, d), BF16)],
        compiler_params=_params(("parallel", "arbitrary")),
        name="in_proj",
    )(x, g.reshape(1, d), mod, w, cos, sa, sb)


def _attn_kernel(lam_ref, q_ref, k_ref, v_ref, g_ref, o_ref, s_ref, vt_ref, *, tq, tk, ctx_tiles, ctx_chunks,
                 all_chunks, out_scale):
    qi = pl.program_id(2)
    hp = q_ref.shape[1] // LANES
    heads = [slice(h * LANES, (h + 1) * LANES) for h in range(hp)]

    @pl.when(qi == 0)
    def _():
        for h, hs in enumerate(heads):
            for c in range(all_chunks):
                vt_ref[h, c] = v_ref[c * tk:(c + 1) * tk, hs].astype(F32).T.astype(BF16)

    row = lax.broadcasted_iota(jnp.int32, (LANES, tq), 0)
    q2ts = []
    for hs in heads:
        qt = q_ref[:, hs].astype(F32).T
        q2ts.append(jnp.concatenate([jnp.where(row < ATT_QK_DIM, qt, 0.0), jnp.where(row >= ATT_QK_DIM, qt, 0.0)],
                                    axis=1).astype(BF16))

    def fold(x):
        return x.reshape(tk // 8, 8, x.shape[-1])

    def scores(h, n_chunks):
        mm = None
        for c in range(n_chunks):
            s = jnp.dot(k_ref[c * tk:(c + 1) * tk, heads[h]], q2ts[h], preferred_element_type=F32)
            s_ref[h, c] = s
            sf = jnp.max(fold(s), axis=0)
            mm = sf if mm is None else jnp.maximum(mm, sf)
        return jnp.max(mm, axis=0, keepdims=True)

    def values(h, n_chunks, m):
        ps = None
        acc = None
        for c in range(n_chunks):
            p = jnp.exp2(s_ref[h, c] - m)
            pf = jnp.sum(fold(p), axis=0)
            ps = pf if ps is None else ps + pf
            pv = jnp.dot(vt_ref[h, c], p.astype(BF16), preferred_element_type=F32)
            acc = pv if acc is None else acc + pv
        ot = acc * (1.0 / jnp.sum(ps, axis=0, keepdims=True))
        o = (ot[:, :tq] - lam_ref[0] * ot[:, tq:]).T
        ms = jnp.mean(o * o, axis=-1, keepdims=True)
        o = (o * lax.rsqrt(ms + EPS)) * g_ref[...] * out_scale
        o_ref[:, heads[h]] = o.astype(BF16)

    def attend(n_chunks):
        ms = [scores(h, n_chunks) for h in range(hp)]
        for h in range(hp):
            values(h, n_chunks, ms[h])

    @pl.when(qi < ctx_tiles)
    def _():
        attend(ctx_chunks)

    @pl.when(qi >= ctx_tiles)
    def _():
        attend(all_chunks)


def _attention(p3, lam, subln_g, lambda_init, ctx_len):
    b, t, _ = p3.shape
    tq = SUB
    tk = SUB
    hp = ATT_HEADS_PER_STEP
    w = hp * LANES
    kern = functools.partial(_attn_kernel, tq=tq, tk=tk, ctx_tiles=ctx_len // tq, ctx_chunks=ctx_len // tk,
                             all_chunks=t // tk, out_scale=1.0 - lambda_init)
    return pl.pallas_call(
        kern,
        grid=(b, ATT_HEADS // hp, t // tq),
        in_specs=[
            pl.BlockSpec(memory_space=pltpu.SMEM),
            pl.BlockSpec((None, tq, w), lambda bi, h, qi: (bi, qi, COL_Q // w + h)),
            pl.BlockSpec((None, t, w), lambda bi, h, qi: (bi, 0, COL_K // w + h)),
            pl.BlockSpec((None, t, w), lambda bi, h, qi: (bi, 0, COL_V // w + h)),
            pl.BlockSpec((1, LANES), lambda bi, h, qi: (0, 0)),
        ],
        out_specs=pl.BlockSpec((None, tq, w), lambda bi, h, qi: (bi, qi, h)),
        out_shape=jax.ShapeDtypeStruct((b, t, ATT_WIDTH), BF16),
        scratch_shapes=[pltpu.VMEM((hp, t // tk, tk, 2 * tq), F32), pltpu.VMEM((hp, t // tk, LANES, tk), BF16)],
        compiler_params=_params(("parallel", "parallel", "arbitrary")),
        name="diff_attention",
    )(lam, p3, p3, p3, subln_g.reshape(1, LANES))


CONV_HALO = 16


def _conv_kernel(prev_ref, cur_ref, next_ref, w_ref, b_ref, o_ref, *, rb, ctx_blocks, n_blocks):
    i = pl.program_id(1)
    prev_ok = jnp.logical_and(i != 0, i != ctx_blocks)
    next_ok = jnp.logical_and(i != ctx_blocks - 1, i != n_blocks - 1)
    prev = jnp.where(prev_ok, prev_ref[...].astype(F32), 0.0)
    nxt = jnp.where(next_ok, next_ref[...].astype(F32), 0.0)
    xc = jnp.concatenate([prev, cur_ref[...].astype(F32), nxt], axis=0)
    pad = SSM_CONV // 2
    acc = b_ref[...] + w_ref[0:1, :] * xc[CONV_HALO - pad:CONV_HALO - pad + rb, :]
    for k in range(1, SSM_CONV):
        s = CONV_HALO - pad + k
        acc = acc + w_ref[k:k + 1, :] * xc[s:s + rb, :]
    o_ref[...] = _silu(acc).astype(BF16)


def _conv_silu(p3, conv_w, conv_b, ctx_len):
    b, t, _ = p3.shape
    rb = SUB
    tc = 512
    hb = rb // CONV_HALO
    n_blocks = t // rb
    c0 = COL_XBC // tc
    kern = functools.partial(_conv_kernel, rb=rb, ctx_blocks=ctx_len // rb, n_blocks=n_blocks)
    last_halo = t // CONV_HALO - 1
    return pl.pallas_call(
        kern,
        grid=(b, n_blocks, XBC_COLS // tc),
        in_specs=[
            pl.BlockSpec((None, CONV_HALO, tc), lambda bi, i, j: (bi, jnp.maximum(i * hb - 1, 0), c0 + j)),
            pl.BlockSpec((None, rb, tc), lambda bi, i, j: (bi, i, c0 + j)),
            pl.BlockSpec((None, CONV_HALO, tc), lambda bi, i, j: (bi, jnp.minimum((i + 1) * hb, last_halo), c0 + j)),
            pl.BlockSpec((SSM_CONV, tc), lambda bi, i, j: (0, j)),
            pl.BlockSpec((1, tc), lambda bi, i, j: (0, j)),
        ],
        out_specs=pl.BlockSpec((None, rb, tc), lambda bi, i, j: (bi, i, j)),
        out_shape=jax.ShapeDtypeStruct((b, t, XBC_COLS), BF16),
        compiler_params=_params(("parallel", "parallel", "parallel")),
        name="conv_silu",
    )(p3, p3, p3, conv_w, conv_b.reshape(1, XBC_COLS))


def _split_bf16(a, pieces):
    parts = []
    r = a
    for _ in range(pieces):
        p = r.astype(BF16)
        parts.append(p)
        r = r - p.astype(F32)
    return parts


def _dot_sel(a, sel, pieces):
    out = None
    for p in _split_bf16(a, pieces):
        t = jnp.dot(p, sel, preferred_element_type=F32)
        out = t if out is None else out + t
    return out


def _sel_dot(sel, a, pieces):
    out = None
    for p in _split_bf16(a, pieces):
        t = jnp.dot(sel, p, preferred_element_type=F32)
        out = t if out is None else out + t
    return out


def _ssd_chunk(xs, bm, cm, dtb, bias, alog, h_ref, tri, ex64, reverse):
    q = xs.shape[0]
    hp = SSM_HEADS_PER_GROUP
    dt = jax.nn.softplus(dtb.astype(F32) + bias)
    a = dt * (-jnp.exp(alog))
    acs = _sel_dot(tri, a, 3)
    acs_t = acs.T
    end = acs[0:1, :] if reverse else acs[q - 1:q, :]
    dt_e = _dot_sel(dt, ex64, 2)
    eacs_e = _dot_sel(jnp.exp(acs), ex64, 2)
    dte_e = _dot_sel(jnp.exp(end - acs), ex64, 2)
    xsf = xs.astype(F32)
    xdt = xsf * dt_e
    cb = lax.dot_general(cm, bm, (((1,), (1,)), ((), ())), preferred_element_type=F32)
    qi = lax.broadcasted_iota(jnp.int32, (q, q), 0)
    si = lax.broadcasted_iota(jnp.int32, (q, q), 1)
    mask = (si >= qi) if reverse else (si <= qi)
    lane = lax.broadcasted_iota(jnp.int32, (q, LANES), 1)
    ys = []
    for j in range(hp // 2):
        xp = xdt[:, j * LANES:(j + 1) * LANES]
        halves = (jnp.where(lane < SSM_HEAD_DIM, xp, 0.0).astype(BF16),
                  jnp.where(lane >= SSM_HEAD_DIM, xp, 0.0).astype(BF16))
        acc = None
        for k in range(2):
            r = 2 * j + k
            seg = jnp.broadcast_to(acs[:, r:r + 1], (q, LANES)) - acs_t[r:r + 1, :]
            dec = jnp.exp(jnp.where(mask, seg, -1e30))
            t = jnp.dot((cb * dec).astype(BF16), halves[k], preferred_element_type=F32)
            acc = t if acc is None else acc + t
        ys.append(acc)
    y_diag = jnp.concatenate(ys, axis=1)
    h = h_ref[...]
    y_off = jnp.dot(cm, h.astype(BF16), preferred_element_type=F32) * eacs_e
    cdec = eacs_e[0:1, :] if reverse else eacs_e[q - 1:q, :]
    xd = (xdt * dte_e).astype(BF16)
    h_ref[...] = h * cdec + lax.dot_general(bm, xd, (((0,), (0,)), ((), ())), preferred_element_type=F32)
    return y_diag + y_off, xsf


def _ssd_kernel(xsf_ref, bmf_ref, cmf_ref, dtf_ref, xsb_ref, bmb_ref, cmb_ref, dtb_ref,
                biasf_ref, biasb_ref, alogf_ref, alogb_ref, dskip_ref, tril_ref, triu_ref, ex64_ref,
                yf_ref, yb_ref, hf_ref, hb_ref):
    @pl.when(pl.program_id(2) == 0)
    def _():
        hf_ref[...] = jnp.zeros_like(hf_ref)
        hb_ref[...] = jnp.zeros_like(hb_ref)

    yf, xsf = _ssd_chunk(xsf_ref[...], bmf_ref[...], cmf_ref[...], dtf_ref[...], biasf_ref[...], alogf_ref[...],
                         hf_ref, tril_ref[...], ex64_ref[...], False)
    yf_ref[...] = yf + dskip_ref[...] * xsf
    yb, _ = _ssd_chunk(xsb_ref[...], bmb_ref[...], cmb_ref[...], dtb_ref[...], biasb_ref[...], alogb_ref[...],
                       hb_ref, triu_ref[...], ex64_ref[...], True)
    yb_ref[...] = yb


def _ssd_scan(u3, p3, bias4, alog4, dskip, ctx_len):
    b, t, _ = u3.shape
    q = SSM_CHUNK
    ns = t // q
    ncc = ctx_len // q
    g_n = SSM_GROUPS

    def fwd(s):
        return s

    def bwd(s):
        return jnp.where(s < ncc, ncc - 1 - s, ns - 1 + ncc - s)

    xs_blk = GROUP_WIDTH // LANES
    b_col = SSM_WIDTH // LANES
    c_col = b_col + g_n
    dt_col = COL_DT // LANES
    del xs_blk

    def specs(cidx, direction):
        return [
            pl.BlockSpec((None, q, GROUP_WIDTH), lambda bi, g, s: (bi, cidx(s), g)),
            pl.BlockSpec((None, q, LANES), lambda bi, g, s: (bi, cidx(s), b_col + g)),
            pl.BlockSpec((None, q, LANES), lambda bi, g, s: (bi, cidx(s), c_col + g)),
            pl.BlockSpec((None, q, LANES), lambda bi, g, s: (bi, cidx(s), dt_col + direction * g_n + g)),
        ]

    row = lambda direction: pl.BlockSpec((None, 1, LANES), lambda bi, g, s: (direction * g_n + g, 0, 0))
    full = lambda arr: pl.BlockSpec(arr.shape, lambda bi, g, s: (0,) * arr.ndim)

    ri = lax.broadcasted_iota(jnp.int32, (q, q), 0)
    ci = lax.broadcasted_iota(jnp.int32, (q, q), 1)
    tril = (ci <= ri).astype(BF16)
    triu = (ci >= ri).astype(BF16)
    r64 = lax.broadcasted_iota(jnp.int32, (LANES, GROUP_WIDTH), 0)
    c64 = lax.broadcasted_iota(jnp.int32, (LANES, GROUP_WIDTH), 1)
    ex64 = (c64 // SSM_HEAD_DIM == r64).astype(BF16)

    in_specs = (specs(fwd, 0) + specs(bwd, 1) + [row(0), row(1), row(0), row(1),
                pl.BlockSpec((None, 1, GROUP_WIDTH), lambda bi, g, s: (g, 0, 0)),
                full(tril), full(triu), full(ex64)])
    out_spec = lambda cidx: pl.BlockSpec((None, q, GROUP_WIDTH), lambda bi, g, s: (bi, cidx(s), g))
    return pl.pallas_call(
        _ssd_kernel,
        grid=(b, g_n, ns),
        in_specs=in_specs,
        out_specs=[out_spec(fwd), out_spec(bwd)],
        out_shape=[jax.ShapeDtypeStruct((b, t, SSM_WIDTH), F32)] * 2,
        scratch_shapes=[pltpu.VMEM((SSM_STATE, GROUP_WIDTH), F32)] * 2,
        compiler_params=_params(("parallel", "parallel", "arbitrary")),
        name="ssd_scan",
    )(u3, u3, u3, p3, u3, u3, u3, p3, bias4, bias4, alog4, alog4, dskip, tril, triu, ex64)


def _ssd_finish_kernel(yf_ref, yb_ref, z_ref, g_ref, o_ref):
    y = (yf_ref[...] + yb_ref[...]) * _silu(z_ref[...].astype(F32))
    outs = []
    for g in range(SSM_GROUPS):
        yg = y[:, g * GROUP_WIDTH:(g + 1) * GROUP_WIDTH]
        ms = jnp.mean(yg * yg, axis=-1, keepdims=True)
        outs.append((yg * lax.rsqrt(ms + EPS)) * g_ref[:, g * GROUP_WIDTH:(g + 1) * GROUP_WIDTH])
    o_ref[...] = jnp.concatenate(outs, axis=1).astype(BF16)


def _ssd_finish(yf, yb, p, norm_g):
    m = yf.shape[0]
    tm = _pick(m, (512, 256))
    return pl.pallas_call(
        _ssd_finish_kernel,
        grid=(m // tm,),
        in_specs=[
            pl.BlockSpec((tm, SSM_WIDTH), lambda i: (i, 0)),
            pl.BlockSpec((tm, SSM_WIDTH), lambda i: (i, 0)),
            pl.BlockSpec((tm, SSM_WIDTH), lambda i: (i, COL_Z // SSM_WIDTH)),
            pl.BlockSpec((1, SSM_WIDTH), lambda i: (0, 0)),
        ],
        out_specs=pl.BlockSpec((tm, SSM_WIDTH), lambda i: (i, 0)),
        out_shape=jax.ShapeDtypeStruct((m, SSM_WIDTH), BF16),
        compiler_params=_params(("parallel",)),
        name="ssd_finish",
    )(yf, yb, p, norm_g.reshape(1, SSM_WIDTH))


def _mm_resid_kernel(*refs, n_a, tm, geo):
    a_refs = refs[:n_a]
    w_refs = refs[n_a:2 * n_a]
    x_ref, gate_ref, o_ref = refs[2 * n_a:]
    i = pl.program_id(0)
    acc = None
    for a_ref, w_ref in zip(a_refs, w_refs):
        t = jnp.dot(a_ref[...], w_ref[...], preferred_element_type=F32)
        acc = t if acc is None else acc + t
    for u in range(tm // SUB):
        row = _mod_row(i * (tm // SUB) + u, geo)
        rs = slice(u * SUB, (u + 1) * SUB)
        o_ref[rs, :] = x_ref[rs, :] + gate_ref[pl.ds(row, 1), :] * acc[rs, :]


def _mm_resid(a_list, w_list, x, mod, gate_idx, geo, tn):
    m, d = x.shape
    tm = _pick(m, (1024, 512, 256))
    n_a = len(a_list)
    kern = functools.partial(_mm_resid_kernel, n_a=n_a, tm=tm, geo=geo)
    gate_blk = gate_idx * d // tn
    in_specs = ([pl.BlockSpec((tm, a.shape[1]), lambda i, j: (i, 0)) for a in a_list]
                + [pl.BlockSpec((w.shape[0], tn), lambda i, j: (0, j)) for w in w_list]
                + [pl.BlockSpec((tm, tn), lambda i, j: (i, j)),
                   pl.BlockSpec((mod.shape[0], tn), lambda i, j: (0, gate_blk + j))])
    return pl.pallas_call(
        kern,
        grid=(m // tm, d // tn),
        in_specs=in_specs,
        out_specs=pl.BlockSpec((tm, tn), lambda i, j: (i, j)),
        out_shape=jax.ShapeDtypeStruct((m, d), F32),
        compiler_params=_params(("parallel", "parallel")),
        name="matmul_gated_residual",
    )(*a_list, *w_list, x, mod)


def _ffn_in_kernel(x_ref, g_ref, mod_ref, w1_ref, w3_ref, o_ref, h_ref, *, tm, d, geo):
    i = pl.program_id(0)

    @pl.when(pl.program_id(1) == 0)
    def _():
        for u in range(tm // SUB):
            row = _mod_row(i * (tm // SUB) + u, geo)
            rs = slice(u * SUB, (u + 1) * SUB)
            shift = mod_ref[pl.ds(row, 1), 3 * d:4 * d]
            scale = mod_ref[pl.ds(row, 1), 4 * d:5 * d]
            h_ref[rs, :] = _normmod(x_ref[rs, :], g_ref[...], shift, scale).astype(BF16)

    h = h_ref[...]
    a = jnp.dot(h, w1_ref[...], preferred_element_type=F32)
    b = jnp.dot(h, w3_ref[...], preferred_element_type=F32)
    o_ref[...] = (_silu(a) * b).astype(BF16)


def _ffn_in(x, g, mod, w1, w3, geo):
    m, d = x.shape
    ff = w1.shape[1]
    tm = _pick(m, (1024, 512, 256))
    tf = _pick(ff, (512, 256, 128))
    kern = functools.partial(_ffn_in_kernel, tm=tm, d=d, geo=geo)
    return pl.pallas_call(
        kern,
        grid=(m // tm, ff // tf),
        in_specs=[
            pl.BlockSpec((tm, d), lambda i, j: (i, 0)),
            pl.BlockSpec((1, d), lambda i, j: (0, 0)),
            pl.BlockSpec(mod.shape, lambda i, j: (0, 0)),
            pl.BlockSpec((d, tf), lambda i, j: (0, j)),
            pl.BlockSpec((d, tf), lambda i, j: (0, j)),
        ],
        out_specs=pl.BlockSpec((tm, tf), lambda i, j: (i, j)),
        out_shape=jax.ShapeDtypeStruct((m, ff), BF16),
        scratch_shapes=[pltpu.VMEM((tm, d), BF16)],
        compiler_params=_params(("parallel", "arbitrary")),
        name="ffn_in",
    )(x, g.reshape(1, d), mod, w1, w3)


def _store_row_tiles(dst_ref, src_ref, rt):
    tm = src_ref.shape[0]
    for a in range(rt):
        dst_ref[pl.ds(a, tm, stride=rt), :] = src_ref[:, a * LANES:(a + 1) * LANES]


def _load_row_tiles(src_ref, row0, tm, rt):
    return jnp.concatenate([src_ref[pl.ds(row0 * rt + a, tm, stride=rt), :] for a in range(rt)], axis=1)


def _router_kernel(x_ref, g_ref, mod_ref, rw_ref, h_ref, r_ref, hs_ref, *, tm, d, n_exp, geo):
    i = pl.program_id(0)
    for u in range(tm // SUB):
        row = _mod_row(i * (tm // SUB) + u, geo)
        rs = slice(u * SUB, (u + 1) * SUB)
        shift = mod_ref[pl.ds(row, 1), 3 * d:4 * d]
        scale = mod_ref[pl.ds(row, 1), 4 * d:5 * d]
        hs_ref[rs, :] = _normmod(x_ref[rs, :], g_ref[...], shift, scale)
    _store_row_tiles(h_ref, hs_ref, d // LANES)
    logits = jnp.dot(hs_ref[...].astype(BF16), rw_ref[...], preferred_element_type=F32)
    lane = lax.broadcasted_iota(jnp.int32, logits.shape, 1)
    neg = jnp.float32(-1e30)
    logits = jnp.where(lane < n_exp, logits, neg)
    m1 = jnp.max(logits, axis=-1, keepdims=True)
    i1 = jnp.min(jnp.where(logits == m1, lane, LANES), axis=-1, keepdims=True)
    rest = jnp.where(lane == i1, neg, logits)
    m2 = jnp.max(rest, axis=-1, keepdims=True)
    i2 = jnp.min(jnp.where(rest == m2, lane, LANES), axis=-1, keepdims=True)
    t = jnp.exp(m2 - m1)
    g1 = 1.0 / (1.0 + t)
    g2 = t * g1
    out = jnp.where(lane == 0, i1.astype(F32),
                    jnp.where(lane == 1, i2.astype(F32),
                              jnp.where(lane == 2, g1, jnp.where(lane == 3, g2, 0.0))))
    r_ref[...] = out


def _router(x, g, mod, rw_pad, n_exp, geo):
    m, d = x.shape
    rt = d // LANES
    tm = _pick(m, (512, 256))
    kern = functools.partial(_router_kernel, tm=tm, d=d, n_exp=n_exp, geo=geo)
    return pl.pallas_call(
        kern,
        grid=(m // tm,),
        in_specs=[
            pl.BlockSpec((tm, d), lambda i: (i, 0)),
            pl.BlockSpec((1, d), lambda i: (0, 0)),
            pl.BlockSpec(mod.shape, lambda i: (0, 0)),
            pl.BlockSpec((d, LANES), lambda i: (0, 0)),
        ],
        out_specs=[pl.BlockSpec((tm * rt, LANES), lambda i: (i, 0)), pl.BlockSpec((tm, LANES), lambda i: (i, 0))],
        out_shape=[jax.ShapeDtypeStruct((m * rt, LANES), F32), jax.ShapeDtypeStruct((m, LANES), F32)],
        scratch_shapes=[pltpu.VMEM((tm, d), F32)],
        compiler_params=_params(("parallel",)),
        name="moe_router",
    )(x, g.reshape(1, d), mod, rw_pad)


GATHER_ROWS = 512
GATHER_UNROLL = 8


def _gather_kernel(*refs, n_out, rows, rt):
    idx_refs = refs[:n_out]
    src_ref = refs[n_out]
    dst_refs = refs[n_out + 1:2 * n_out + 1]
    sem = refs[2 * n_out + 1]

    def copy(k, r):
        src_row = pl.multiple_of(idx_refs[k][r] * rt, rt)
        dst_row = pl.multiple_of(r * rt, rt)
        return pltpu.make_async_copy(src_ref.at[pl.ds(src_row, rt), :], dst_refs[k].at[pl.ds(dst_row, rt), :], sem)

    def start(b, c):
        for u in range(GATHER_UNROLL):
            for k in range(n_out):
                copy(k, b * GATHER_UNROLL + u).start()
        return c

    def wait(b, c):
        for u in range(GATHER_UNROLL):
            for k in range(n_out):
                copy(k, b * GATHER_UNROLL + u).wait()
        return c

    lax.fori_loop(0, rows // GATHER_UNROLL, start, 0)
    lax.fori_loop(0, rows // GATHER_UNROLL, wait, 0)


def _gather_rows(src, idx_list, rt):
    n = idx_list[0].shape[0]
    rows = GATHER_ROWS
    assert n % rows == 0
    n_out = len(idx_list)
    kern = functools.partial(_gather_kernel, n_out=n_out, rows=rows, rt=rt)
    outs = pl.pallas_call(
        kern,
        grid=(n // rows,),
        in_specs=([pl.BlockSpec((rows,), lambda i: (i,), memory_space=pltpu.SMEM) for _ in idx_list]
                  + [pl.BlockSpec(memory_space=pl.ANY)]),
        out_specs=[pl.BlockSpec((rows * rt, LANES), lambda i: (i, 0)) for _ in idx_list],
        out_shape=[jax.ShapeDtypeStruct((n * rt, LANES), src.dtype) for _ in idx_list],
        scratch_shapes=[pltpu.SemaphoreType.DMA(())],
        compiler_params=_params(("arbitrary",), has_side_effects=True),
        name="gather_rows",
    )(*idx_list, src)
    return outs


def _moe_in_kernel(te_ref, nu_ref, h_ref, w1_ref, w3_ref, o_ref, hb_ref):
    t = pl.program_id(0)
    j = pl.program_id(1)
    tm, d = hb_ref.shape

    @pl.when(t < nu_ref[0])
    def _():
        @pl.when(j == 0)
        def _():
            hb_ref[...] = _load_row_tiles(h_ref, 0, tm, d // LANES).astype(BF16)

        h = hb_ref[...]
        a = jnp.dot(h, w1_ref[...], preferred_element_type=F32)
        b = jnp.dot(h, w3_ref[...], preferred_element_type=F32)
        o_ref[...] = (_silu(a) * b).astype(BF16)

    @pl.when(t >= nu_ref[0])
    def _():
        o_ref[...] = jnp.zeros_like(o_ref)


def _moe_in(hs, w1, w3, layer, tile_expert, n_used, tm):
    d = w1.shape[2]
    rt = d // LANES
    s = hs.shape[0] // rt
    ff = w1.shape[3]
    tf = _pick(ff, (1408, 512, 256, 128))
    n_ff = ff // tf

    def w_map(t, j, te, nu):
        return (layer, te[t], 0, jnp.where(t < nu[0], j, n_ff - 1))

    return pl.pallas_call(
        _moe_in_kernel,
        grid_spec=pltpu.PrefetchScalarGridSpec(
            num_scalar_prefetch=2,
            grid=(s // tm, n_ff),
            in_specs=[
                pl.BlockSpec((tm * rt, LANES), lambda t, j, te, nu: (t, 0)),
                pl.BlockSpec((None, None, d, tf), w_map),
                pl.BlockSpec((None, None, d, tf), w_map),
            ],
            out_specs=pl.BlockSpec((tm, tf), lambda t, j, te, nu: (t, j)),
            scratch_shapes=[pltpu.VMEM((tm, d), BF16)],
        ),
        out_shape=jax.ShapeDtypeStruct((s, ff), BF16),
        compiler_params=_params(("arbitrary", "arbitrary")),
        name="moe_experts_in",
    )(tile_expert, n_used, hs, w1, w3)


def _moe_out_kernel(te_ref, nu_ref, u_ref, w2_ref, o_ref, *, rt):
    t = pl.program_id(0)
    j = pl.program_id(1)
    tm = u_ref.shape[0]
    cols = w2_ref.shape[1] // LANES
    valid = t < nu_ref[0]

    @pl.when(valid)
    def _():
        y = jnp.dot(u_ref[...], w2_ref[...], preferred_element_type=F32)
        for a in range(cols):
            o_ref[pl.ds(j * cols + a, tm, stride=rt), :] = y[:, a * LANES:(a + 1) * LANES]

    @pl.when(jnp.logical_and(jnp.logical_not(valid), j == 0))
    def _():
        o_ref[...] = jnp.zeros_like(o_ref)


def _moe_out(u, w2, layer, tile_expert, n_used, tm):
    s, ff = u.shape
    d = w2.shape[3]
    rt = d // LANES
    tn = _pick(d, (512, 256, 128))
    n_n = d // tn

    def w_map(t, j, te, nu):
        return (layer, te[t], 0, jnp.where(t < nu[0], j, n_n - 1))

    return pl.pallas_call(
        functools.partial(_moe_out_kernel, rt=rt),
        grid_spec=pltpu.PrefetchScalarGridSpec(
            num_scalar_prefetch=2,
            grid=(s // tm, n_n),
            in_specs=[
                pl.BlockSpec((tm, ff), lambda t, j, te, nu: (t, 0)),
                pl.BlockSpec((None, None, ff, tn), w_map),
            ],
            out_specs=pl.BlockSpec((tm * rt, LANES), lambda t, j, te, nu: (t, 0)),
        ),
        out_shape=jax.ShapeDtypeStruct((s * rt, LANES), F32),
        compiler_params=_params(("arbitrary", "arbitrary")),
        name="moe_experts_out",
    )(tile_expert, n_used, u, w2)


def _moe_combine_kernel(x_ref, ya_ref, yb_ref, route_ref, gate_ref, o_ref, *, tm, geo):
    i = pl.program_id(0)
    rt = x_ref.shape[1] // LANES
    for u in range(tm // SUB):
        row = _mod_row(i * (tm // SUB) + u, geo)
        rs = slice(u * SUB, (u + 1) * SUB)
        g1 = route_ref[rs, TOP_K:TOP_K + 1]
        g2 = route_ref[rs, TOP_K + 1:TOP_K + 2]
        y = g1 * _load_row_tiles(ya_ref, u * SUB, SUB, rt) + g2 * _load_row_tiles(yb_ref, u * SUB, SUB, rt)
        o_ref[rs, :] = x_ref[rs, :] + gate_ref[pl.ds(row, 1), :] * y


def _moe_combine(x, ya, yb, route, mod, geo):
    m, d = x.shape
    rt = d // LANES
    tm = _pick(m, (512, 256))
    kern = functools.partial(_moe_combine_kernel, tm=tm, geo=geo)
    blk = pl.BlockSpec((tm, d), lambda i: (i, 0))
    blk3 = pl.BlockSpec((tm * rt, LANES), lambda i: (i, 0))
    return pl.pallas_call(
        kern,
        grid=(m // tm,),
        in_specs=[blk, blk3, blk3, pl.BlockSpec((tm, LANES), lambda i: (i, 0)),
                  pl.BlockSpec((mod.shape[0], d), lambda i: (0, 5))],
        out_specs=blk,
        out_shape=jax.ShapeDtypeStruct((m, d), F32),
        compiler_params=_params(("parallel",)),
        name="moe_combine",
    )(x, ya, yb, route, mod)


def _moe_plan(route, n_exp, tm, n_tiles):
    m = route.shape[0]
    e = route[:, 0:TOP_K].astype(jnp.int32).reshape(-1)
    onehot = (e[:, None] == jnp.arange(n_exp, dtype=jnp.int32)[None, :]).astype(jnp.int32)
    csum = jnp.cumsum(onehot, axis=0)
    rank = jnp.sum(csum * onehot, axis=1) - 1
    counts = csum[-1]
    tiles_per = (counts + tm - 1) // tm
    tile_end = jnp.cumsum(tiles_per)
    tile_start = tile_end - tiles_per
    pos = tile_start[e] * tm + rank
    n_used = tile_end[-1]
    n_slots = n_tiles * tm
    token = jnp.arange(TOP_K * m, dtype=jnp.int32) // TOP_K
    slot_token = jnp.zeros((n_slots,), jnp.int32).at[pos].set(token)
    tile_ids = jnp.arange(n_tiles, dtype=jnp.int32)
    tile_expert = jnp.sum((tile_ids[:, None] >= tile_end[None, :]).astype(jnp.int32), axis=1)
    last_expert = jnp.sum((n_used - 1 >= tile_end).astype(jnp.int32))
    tile_expert = jnp.where(tile_ids < n_used, tile_expert, last_expert).astype(jnp.int32)
    return pos.astype(jnp.int32), slot_token, tile_expert, n_used.reshape(1).astype(jnp.int32)


def _moe_layer(x, g, mod, rw_pad, w1, w3, w2, layer, geo):
    m, d = x.shape
    rt = d // LANES
    n_exp = w1.shape[1]
    tm = 512
    n_tiles = -(-(TOP_K * m) // tm) + n_exp
    n_tiles = -(-(n_tiles * tm) // GATHER_ROWS) * GATHER_ROWS // tm
    h, route = _router(x, g, mod, rw_pad, n_exp, geo)
    pos, slot_token, tile_expert, n_used = _moe_plan(route, n_exp, tm, n_tiles)
    (hs,) = _gather_rows(h, [slot_token], rt)
    u = _moe_in(hs, w1, w3, layer, tile_expert, n_used, tm)
    y = _moe_out(u, w2, layer, tile_expert, n_used, tm)
    mp = -(-m // GATHER_ROWS) * GATHER_ROWS
    pos2 = jnp.pad(pos.reshape(m, TOP_K), ((0, mp - m), (0, 0)))
    ya, yb = _gather_rows(y, [pos2[:, 0], pos2[:, 1]], rt)
    return _moe_combine(x, ya[:m * rt], yb[:m * rt], route, mod, geo)


def _final_norm_kernel(x_ref, g_ref, o_ref):
    x = x_ref[...]
    ms = jnp.mean(x * x, axis=-1, keepdims=True)
    o_ref[...] = (x * lax.rsqrt(ms + EPS)) * g_ref[...]


def _final_norm(x3, g, ctx_len):
    b, t, d = x3.shape
    seq = t - ctx_len
    off = ctx_len // SUB
    return pl.pallas_call(
        _final_norm_kernel,
        grid=(b, seq // SUB),
        in_specs=[pl.BlockSpec((None, SUB, d), lambda bi, i: (bi, off + i, 0)),
                  pl.BlockSpec((1, d), lambda bi, i: (0, 0))],
        out_specs=pl.BlockSpec((None, SUB, d), lambda bi, i: (bi, i, 0)),
        out_shape=jax.ShapeDtypeStruct((b, seq, d), F32),
        compiler_params=_params(("parallel", "parallel")),
        name="final_norm",
    )(x3, g.reshape(1, d))


def _rope_tables(batch, seq, ctx_len):
    rows = seq // GRID_W
    row_pos = jnp.repeat(jnp.arange(rows, dtype=F32), GRID_W)
    col_pos = jnp.tile(jnp.arange(GRID_W, dtype=F32), rows)
    inv_freq = ROPE_THETA ** (-jnp.arange(ROPE_PAIRS_PER_AXIS, dtype=F32) / ROPE_PAIRS_PER_AXIS)
    ang = jnp.concatenate([row_pos[:, None] * inv_freq, col_pos[:, None] * inv_freq], axis=-1)
    cos = jnp.cos(ang)
    sin = jnp.sin(ang)
    zero = jnp.zeros_like(sin)
    cos128 = jnp.tile(cos, (1, 4))
    sa128 = jnp.tile(jnp.concatenate([-sin, zero], axis=-1), (1, 2))
    sb128 = jnp.tile(jnp.concatenate([zero, sin], axis=-1), (1, 2))

    def stream(tab, ctx_val):
        ctx_rows = jnp.full((ctx_len, LANES), ctx_val, F32)
        return jnp.tile(jnp.concatenate([ctx_rows, tab], axis=0), (batch, 1))

    return stream(cos128, 1.0), stream(sa128, 0.0), stream(sb128, 0.0)


def _pad_w_in(w_in):
    depth, d, _ = w_in.shape
    main = w_in[:, :, :COL_DT]
    dt = w_in[:, :, COL_DT:COL_DT + 2 * SSM_HEADS].reshape(depth, d, 2 * SSM_GROUPS, SSM_HEADS_PER_GROUP)
    dt = jnp.pad(dt, ((0, 0), (0, 0), (0, 0), (0, LANES - SSM_HEADS_PER_GROUP)))
    return jnp.concatenate([main, dt.reshape(depth, d, 2 * SSM_GROUPS * LANES)], axis=-1).astype(BF16)


def _lane_rows(v):
    depth = v.shape[0]
    v = v.reshape(depth, 2 * SSM_GROUPS, 1, SSM_HEADS_PER_GROUP)
    return jnp.pad(v, ((0, 0), (0, 0), (0, 0), (0, LANES - SSM_HEADS_PER_GROUP)))


def kernel(x, c, ctx, c_ctx, w_mod, b_mod, norm_mix_g, w_in, lambda_q1, lambda_k1, lambda_q2, lambda_k2, subln_g, conv_w, conv_b, a_log, dt_bias, d_skip, ssm_norm_g, w_out, norm_ffn_g, ffn_w1, ffn_w3, ffn_w2, router_w, moe_w1, moe_w3, moe_w2, final_g):
    batch, seq, d = x.shape
    ctx_len = ctx.shape[1]
    depth = w_mod.shape[0]
    t = ctx_len + seq
    m = batch * t
    assert ctx_len % SUB == 0 and seq % SUB == 0 and seq % GRID_W == 0 and batch < 8
    geo = (t // SUB, ctx_len // SUB, batch)

    w_in_p = _pad_w_in(w_in)
    w_out_b = w_out.astype(BF16)
    moe_w1_b, moe_w3_b, moe_w2_b = moe_w1.astype(BF16), moe_w3.astype(BF16), moe_w2.astype(BF16)
    n_exp = router_w.shape[-1]
    rw_pad = jnp.pad(router_w, ((0, 0), (0, 0), (0, LANES - n_exp))).astype(BF16)

    c_all = jnp.zeros((8, d), F32).at[:batch].set(c).at[batch].set(c_ctx)
    mod_all = _mod_table(c_all, w_mod, b_mod)
    cos, sa, sb = _rope_tables(batch, seq, ctx_len)
    bias_rows = _lane_rows(dt_bias)
    alog_rows = _lane_rows(a_log)
    dskip_rows = jnp.repeat(d_skip, SSM_HEAD_DIM, axis=-1).reshape(depth, SSM_GROUPS, 1, GROUP_WIDTH)

    xs = jnp.concatenate([ctx, x], axis=1).reshape(m, d)
    for li in range(depth):
        lambda_init = 0.8 - 0.6 * math.exp(-0.3 * li)
        mod = mod_all[li]
        lam = (jnp.exp(jnp.sum(lambda_q1[li] * lambda_k1[li])) - jnp.exp(jnp.sum(lambda_q2[li] * lambda_k2[li]))
               + lambda_init).reshape(1).astype(F32)

        p = _in_proj(xs, norm_mix_g[li], mod, w_in_p[li], cos, sa, sb, geo)
        p3 = p.reshape(batch, t, IN_COLS_PAD)
        att = _attention(p3, lam, subln_g[li], lambda_init, ctx_len)
        u3 = _conv_silu(p3, conv_w[li], conv_b[li], ctx_len)
        yf, yb = _ssd_scan(u3, p3, bias_rows[li], alog_rows[li], dskip_rows[li], ctx_len)
        ssm = _ssd_finish(yf.reshape(m, SSM_WIDTH), yb.reshape(m, SSM_WIDTH), p, ssm_norm_g[li])
        xs = _mm_resid([att.reshape(m, ATT_WIDTH), ssm], [w_out_b[li, :ATT_WIDTH], w_out_b[li, ATT_WIDTH:]],
                       xs, mod, 2, geo, tn=512)

        j = li // 2
        if li % 2 == 0:
            uu = _ffn_in(xs, norm_ffn_g[li], mod, ffn_w1[j].astype(BF16), ffn_w3[j].astype(BF16), geo)
            xs = _mm_resid([uu], [ffn_w2[j].astype(BF16)], xs, mod, 5, geo, tn=256)
        else:
            xs = _moe_layer(xs, norm_ffn_g[li], mod, rw_pad[j], moe_w1_b, moe_w3_b, moe_w2_b, j, geo)
    return _final_norm(xs.reshape(batch, t, d), final_g, ctx_len)
```

```python
import functools
import math

import jax
import jax.numpy as jnp
from jax import lax
from jax.experimental import pallas as pl
from jax.experimental.pallas import tpu as pltpu

F32 = jnp.float32
BF16 = jnp.bfloat16

GRID_W = 64
EPS = 1e-6
N_MOD = 6
ATT_HEADS = 8
ATT_QK_DIM = 64
ATT_V_DIM = 128
ATT_WIDTH = ATT_HEADS * ATT_V_DIM
ROPE_THETA = 10000.0
ROPE_PAIRS_PER_AXIS = ATT_QK_DIM // 4
SSM_HEAD_DIM = 64
SSM_GROUPS = 2
SSM_HEADS_PER_GROUP = 8
SSM_HEADS = SSM_GROUPS * SSM_HEADS_PER_GROUP
SSM_WIDTH = SSM_HEADS * SSM_HEAD_DIM
SSM_STATE = 128
SSM_CONV = 5
SSM_CHUNK = 128
GROUP_WIDTH = SSM_WIDTH // SSM_GROUPS
XBC_COLS = SSM_WIDTH + 2 * SSM_GROUPS * SSM_STATE
TOP_K = 2

LANES = 128
SUB = 256
ATT_HEADS_PER_STEP = 2
VMEM_LIMIT = 52 * 1024 * 1024

COL_Q = 0
COL_K = COL_Q + ATT_WIDTH
COL_V = COL_K + ATT_WIDTH
COL_Z = COL_V + ATT_WIDTH
COL_XBC = COL_Z + SSM_WIDTH
COL_DT = COL_XBC + XBC_COLS
IN_COLS_PAD = COL_DT + 2 * SSM_GROUPS * LANES


def _pick(n, cands):
    for c in cands:
        if n % c == 0:
            return c
    raise ValueError(f"no tile in {cands} divides {n}")


def _params(sem, vmem=VMEM_LIMIT, **kw):
    return pltpu.CompilerParams(dimension_semantics=sem, vmem_limit_bytes=vmem, **kw)


def _mod_row(sub_idx, geo):
    subs_per_batch, ctx_subs, n_batch = geo
    b = sub_idx // subs_per_batch
    w = sub_idx % subs_per_batch
    return jnp.where(w < ctx_subs, n_batch, b)


def _normmod(x, g, shift, scale):
    ms = jnp.mean(x * x, axis=-1, keepdims=True)
    hn = (x * lax.rsqrt(ms + EPS)) * g
    return hn * (1.0 + scale) + shift


def _silu(x):
    return x * (1.0 / (1.0 + jnp.exp(-x)))


def _mod_kernel(c_ref, w_ref, b_ref, o_ref):
    s = _silu(c_ref[...]).astype(BF16)
    o_ref[...] = jnp.dot(s, w_ref[...].astype(BF16), preferred_element_type=F32) + b_ref[...]


def _mod_table(c_all, w_mod, b_mod):
    depth, d, n = w_mod.shape
    tn = _pick(n, (1024, 512, 256, 128))
    return pl.pallas_call(
        _mod_kernel,
        grid=(depth, n // tn),
        in_specs=[
            pl.BlockSpec((8, d), lambda l, j: (0, 0)),
            pl.BlockSpec((None, d, tn), lambda l, j: (l, 0, j)),
            pl.BlockSpec((None, 1, tn), lambda l, j: (l, 0, j)),
        ],
        out_specs=pl.BlockSpec((None, 8, tn), lambda l, j: (l, 0, j)),
        out_shape=jax.ShapeDtypeStruct((depth, 8, n), F32),
        compiler_params=_params(("parallel", "parallel")),
        name="mod_table",
    )(c_all, w_mod, b_mod.reshape(depth, 1, n))


def _in_proj_kernel(x_ref, g_ref, mod_ref, w_ref, cos_ref, sa_ref, sb_ref, o_ref, h_ref, *, tm, tn, d, geo):
    i = pl.program_id(0)
    j = pl.program_id(1)

    @pl.when(j == 0)
    def _():
        for u in range(tm // SUB):
            row = _mod_row(i * (tm // SUB) + u, geo)
            rs = slice(u * SUB, (u + 1) * SUB)
            shift = mod_ref[pl.ds(row, 1), 0:d]
            scale = mod_ref[pl.ds(row, 1), d:2 * d]
            h_ref[rs, :] = _normmod(x_ref[rs, :], g_ref[...], shift, scale).astype(BF16)

    acc = jnp.dot(h_ref[...], w_ref[...], preferred_element_type=F32)

    def rope(scale):
        outs = []
        for hs in range(tn // LANES):
            a = acc[:, hs * LANES:(hs + 1) * LANES]
            r = (a * cos_ref[...] + pltpu.roll(a, LANES - 32, 1) * sa_ref[...]
                 + pltpu.roll(a, 32, 1) * sb_ref[...])
            outs.append(r * scale)
        return jnp.concatenate(outs, axis=1)

    @pl.when(j == COL_Q // tn)
    def _():
        o_ref[...] = rope(ATT_QK_DIM ** -0.5 * math.log2(math.e)).astype(BF16)

    @pl.when(j == COL_K // tn)
    def _():
        o_ref[...] = rope(1.0).astype(BF16)

    @pl.when(j >= COL_V // tn)
    def _():
        o_ref[...] = acc.astype(BF16)


def _in_proj(x, g, mod, w, cos, sa, sb, geo):
    m, d = x.shape
    n = w.shape[1]
    tm = _pick(m, (1024, 512, 256))
    tn = ATT_WIDTH
    kern = functools.partial(_in_proj_kernel, tm=tm, tn=tn, d=d, geo=geo)
    return pl.pallas_call(
        kern,
        grid=(m // tm, n // tn),
        in_specs=[
            pl.BlockSpec((tm, d), lambda i, j: (i, 0)),
            pl.BlockSpec((1, d), lambda i, j: (0, 0)),
            pl.BlockSpec(mod.shape, lambda i, j: (0, 0)),
            pl.BlockSpec((d, tn), lambda i, j: (0, j)),
            pl.BlockSpec((tm, LANES), lambda i, j: (i, 0)),
            pl.BlockSpec((tm, LANES), lambda i, j: (i, 0)),
            pl.BlockSpec((tm, LANES), lambda i, j: (i, 0)),
        ],
        out_specs=pl.BlockSpec((tm, tn), lambda i, j: (i, j)),
        out_shape=jax.ShapeDtypeStruct((m, n), BF16),
        scratch_shapes=[pltpu.VMEM((tm, d), BF16)],
        compiler_params=_params(("parallel", "arbitrary")),
        name="in_proj",
    )(x, g.reshape(1, d), mod, w, cos, sa, sb)


def _attn_kernel(lam_ref, q_ref, k_ref, v_ref, g_ref, o_ref, s_ref, vt_ref, *, tq, tk, ctx_tiles, ctx_chunks,
                 all_chunks, out_scale):
    qi = pl.program_id(2)
    hp = q_ref.shape[1] // LANES
    heads = [slice(h * LANES, (h + 1) * LANES) for h in range(hp)]

    @pl.when(qi == 0)
    def _():
        for h, hs in enumerate(heads):
            for c in range(all_chunks):
                vt_ref[h, c] = v_ref[c * tk:(c + 1) * tk, hs].astype(F32).T.astype(BF16)

    row = lax.broadcasted_iota(jnp.int32, (LANES, tq), 0)
    q2ts = []
    for hs in heads:
        qt = q_ref[:, hs].astype(F32).T
        q2ts.append(jnp.concatenate([jnp.where(row < ATT_QK_DIM, qt, 0.0), jnp.where(row >= ATT_QK_DIM, qt, 0.0)],
                                    axis=1).astype(BF16))

    def fold(x):
        return x.reshape(tk // 8, 8, x.shape[-1])

    def scores(h, n_chunks):
        mm = None
        for c in range(n_chunks):
            s = jnp.dot(k_ref[c * tk:(c + 1) * tk, heads[h]], q2ts[h], preferred_element_type=F32)
            s_ref[h, c] = s
            sf = jnp.max(fold(s), axis=0)
            mm = sf if mm is None else jnp.maximum(mm, sf)
        return jnp.max(mm, axis=0, keepdims=True)

    def values(h, n_chunks, m):
        ps = None
        acc = None
        for c in range(n_chunks):
            p = jnp.exp2(s_ref[h, c] - m)
            pf = jnp.sum(fold(p), axis=0)
            ps = pf if ps is None else ps + pf
            pv = jnp.dot(vt_ref[h, c], p.astype(BF16), preferred_element_type=F32)
            acc = pv if acc is None else acc + pv
        ot = acc * (1.0 / jnp.sum(ps, axis=0, keepdims=True))
        o = (ot[:, :tq] - lam_ref[0] * ot[:, tq:]).T
        ms = jnp.mean(o * o, axis=-1, keepdims=True)
        o = (o * lax.rsqrt(ms + EPS)) * g_ref[...] * out_scale
        o_ref[:, heads[h]] = o.astype(BF16)

    def attend(n_chunks):
        ms = [scores(h, n_chunks) for h in range(hp)]
        for h in range(hp):
            values(h, n_chunks, ms[h])

    @pl.when(qi < ctx_tiles)
    def _():
        attend(ctx_chunks)

    @pl.when(qi >= ctx_tiles)
    def _():
        attend(all_chunks)


def _attention(p3, lam, subln_g, lambda_init, ctx_len):
    b, t, _ = p3.shape
    tq = SUB
    tk = SUB
    hp = ATT_HEADS_PER_STEP
    w = hp * LANES
    kern = functools.partial(_attn_kernel, tq=tq, tk=tk, ctx_tiles=ctx_len // tq, ctx_chunks=ctx_len // tk,
                             all_chunks=t // tk, out_scale=1.0 - lambda_init)
    return pl.pallas_call(
        kern,
        grid=(b, ATT_HEADS // hp, t // tq),
        in_specs=[
            pl.BlockSpec(memory_space=pltpu.SMEM),
            pl.BlockSpec((None, tq, w), lambda bi, h, qi: (bi, qi, COL_Q // w + h)),
            pl.BlockSpec((None, t, w), lambda bi, h, qi: (bi, 0, COL_K // w + h)),
            pl.BlockSpec((None, t, w), lambda bi, h, qi: (bi, 0, COL_V // w + h)),
            pl.BlockSpec((1, LANES), lambda bi, h, qi: (0, 0)),
        ],
        out_specs=pl.BlockSpec((None, tq, w), lambda bi, h, qi: (bi, qi, h)),
        out_shape=jax.ShapeDtypeStruct((b, t, ATT_WIDTH), BF16),
        scratch_shapes=[pltpu.VMEM((hp, t // tk, tk, 2 * tq), F32), pltpu.VMEM((hp, t // tk, LANES, tk), BF16)],
        compiler_params=_params(("parallel", "parallel", "arbitrary")),
        name="diff_attention",
    )(lam, p3, p3, p3, subln_g.reshape(1, LANES))


CONV_HALO = 16


def _conv_kernel(prev_ref, cur_ref, next_ref, w_ref, b_ref, o_ref, *, rb, ctx_blocks, n_blocks):
    i = pl.program_id(1)
    prev_ok = jnp.logical_and(i != 0, i != ctx_blocks)
    next_ok = jnp.logical_and(i != ctx_blocks - 1, i != n_blocks - 1)
    prev = jnp.where(prev_ok, prev_ref[...].astype(F32), 0.0)
    nxt = jnp.where(next_ok, next_ref[...].astype(F32), 0.0)
    xc = jnp.concatenate([prev, cur_ref[...].astype(F32), nxt], axis=0)
    pad = SSM_CONV // 2
    acc = b_ref[...] + w_ref[0:1, :] * xc[CONV_HALO - pad:CONV_HALO - pad + rb, :]
    for k in range(1, SSM_CONV):
        s = CONV_HALO - pad + k
        acc = acc + w_ref[k:k + 1, :] * xc[s:s + rb, :]
    o_ref[...] = _silu(acc).astype(BF16)


def _conv_silu(p3, conv_w, conv_b, ctx_len):
    b, t, _ = p3.shape
    rb = SUB
    tc = 512
    hb = rb // CONV_HALO
    n_blocks = t // rb
    c0 = COL_XBC // tc
    kern = functools.partial(_conv_kernel, rb=rb, ctx_blocks=ctx_len // rb, n_blocks=n_blocks)
    last_halo = t // CONV_HALO - 1
    return pl.pallas_call(
        kern,
        grid=(b, n_blocks, XBC_COLS // tc),
        in_specs=[
            pl.BlockSpec((None, CONV_HALO, tc), lambda bi, i, j: (bi, jnp.maximum(i * hb - 1, 0), c0 + j)),
            pl.BlockSpec((None, rb, tc), lambda bi, i, j: (bi, i, c0 + j)),
            pl.BlockSpec((None, CONV_HALO, tc), lambda bi, i, j: (bi, jnp.minimum((i + 1) * hb, last_halo), c0 + j)),
            pl.BlockSpec((SSM_CONV, tc), lambda bi, i, j: (0, j)),
            pl.BlockSpec((1, tc), lambda bi, i, j: (0, j)),
        ],
        out_specs=pl.BlockSpec((None, rb, tc), lambda bi, i, j: (bi, i, j)),
        out_shape=jax.ShapeDtypeStruct((b, t, XBC_COLS), BF16),
        compiler_params=_params(("parallel", "parallel", "parallel")),
        name="conv_silu",
    )(p3, p3, p3, conv_w, conv_b.reshape(1, XBC_COLS))


def _split_bf16(a, pieces):
    parts = []
    r = a
    for _ in range(pieces):
        p = r.astype(BF16)
        parts.append(p)
        r = r - p.astype(F32)
    return parts


def _dot_sel(a, sel, pieces):
    out = None
    for p in _split_bf16(a, pieces):
        t = jnp.dot(p, sel, preferred_element_type=F32)
        out = t if out is None else out + t
    return out


def _sel_dot(sel, a, pieces):
    out = None
    for p in _split_bf16(a, pieces):
        t = jnp.dot(sel, p, preferred_element_type=F32)
        out = t if out is None else out + t
    return out


def _ssd_chunk(xs, bm, cm, dtb, bias, alog, h_ref, tri, ex64, reverse):
    q = xs.shape[0]
    hp = SSM_HEADS_PER_GROUP
    dt = jax.nn.softplus(dtb.astype(F32) + bias)
    a = dt * (-jnp.exp(alog))
    acs = _sel_dot(tri, a, 3)
    acs_t = acs.T
    end = acs[0:1, :] if reverse else acs[q - 1:q, :]
    dt_e = _dot_sel(dt, ex64, 2)
    eacs_e = _dot_sel(jnp.exp(acs), ex64, 2)
    dte_e = _dot_sel(jnp.exp(end - acs), ex64, 2)
    xsf = xs.astype(F32)
    xdt = xsf * dt_e
    cb = lax.dot_general(cm, bm, (((1,), (1,)), ((), ())), preferred_element_type=F32)
    qi = lax.broadcasted_iota(jnp.int32, (q, q), 0)
    si = lax.broadcasted_iota(jnp.int32, (q, q), 1)
    mask = (si >= qi) if reverse else (si <= qi)
    lane = lax.broadcasted_iota(jnp.int32, (q, LANES), 1)
    ys = []
    for j in range(hp // 2):
        xp = xdt[:, j * LANES:(j + 1) * LANES]
        halves = (jnp.where(lane < SSM_HEAD_DIM, xp, 0.0).astype(BF16),
                  jnp.where(lane >= SSM_HEAD_DIM, xp, 0.0).astype(BF16))
        acc = None
        for k in range(2):
            r = 2 * j + k
            seg = jnp.broadcast_to(acs[:, r:r + 1], (q, LANES)) - acs_t[r:r + 1, :]
            dec = jnp.exp(jnp.where(mask, seg, -1e30))
            t = jnp.dot((cb * dec).astype(BF16), halves[k], preferred_element_type=F32)
            acc = t if acc is None else acc + t
        ys.append(acc)
    y_diag = jnp.concatenate(ys, axis=1)
    h = h_ref[...]
    y_off = jnp.dot(cm, h.astype(BF16), preferred_element_type=F32) * eacs_e
    cdec = eacs_e[0:1, :] if reverse else eacs_e[q - 1:q, :]
    xd = (xdt * dte_e).astype(BF16)
    h_ref[...] = h * cdec + lax.dot_general(bm, xd, (((0,), (0,)), ((), ())), preferred_element_type=F32)
    return y_diag + y_off, xsf


def _ssd_kernel(xsf_ref, bmf_ref, cmf_ref, dtf_ref, xsb_ref, bmb_ref, cmb_ref, dtb_ref,
                biasf_ref, biasb_ref, alogf_ref, alogb_ref, dskip_ref, tril_ref, triu_ref, ex64_ref,
                yf_ref, yb_ref, hf_ref, hb_ref):
    @pl.when(pl.program_id(2) == 0)
    def _():
        hf_ref[...] = jnp.zeros_like(hf_ref)
        hb_ref[...] = jnp.zeros_like(hb_ref)

    yf, xsf = _ssd_chunk(xsf_ref[...], bmf_ref[...], cmf_ref[...], dtf_ref[...], biasf_ref[...], alogf_ref[...],
                         hf_ref, tril_ref[...], ex64_ref[...], False)
    yf_ref[...] = yf + dskip_ref[...] * xsf
    yb, _ = _ssd_chunk(xsb_ref[...], bmb_ref[...], cmb_ref[...], dtb_ref[...], biasb_ref[...], alogb_ref[...],
                       hb_ref, triu_ref[...], ex64_ref[...], True)
    yb_ref[...] = yb


def _ssd_scan(u3, p3, bias4, alog4, dskip, ctx_len):
    b, t, _ = u3.shape
    q = SSM_CHUNK
    ns = t // q
    ncc = ctx_len // q
    g_n = SSM_GROUPS

    def fwd(s):
        return s

    def bwd(s):
        return jnp.where(s < ncc, ncc - 1 - s, ns - 1 + ncc - s)

    xs_blk = GROUP_WIDTH // LANES
    b_col = SSM_WIDTH // LANES
    c_col = b_col + g_n
    dt_col = COL_DT // LANES
    del xs_blk

    def specs(cidx, direction):
        return [
            pl.BlockSpec((None, q, GROUP_WIDTH), lambda bi, g, s: (bi, cidx(s), g)),
            pl.BlockSpec((None, q, LANES), lambda bi, g, s: (bi, cidx(s), b_col + g)),
            pl.BlockSpec((None, q, LANES), lambda bi, g, s: (bi, cidx(s), c_col + g)),
            pl.BlockSpec((None, q, LANES), lambda bi, g, s: (bi, cidx(s), dt_col + direction * g_n + g)),
        ]

    row = lambda direction: pl.BlockSpec((None, 1, LANES), lambda bi, g, s: (direction * g_n + g, 0, 0))
    full = lambda arr: pl.BlockSpec(arr.shape, lambda bi, g, s: (0,) * arr.ndim)

    ri = lax.broadcasted_iota(jnp.int32, (q, q), 0)
    ci = lax.broadcasted_iota(jnp.int32, (q, q), 1)
    tril = (ci <= ri).astype(BF16)
    triu = (ci >= ri).astype(BF16)
    r64 = lax.broadcasted_iota(jnp.int32, (LANES, GROUP_WIDTH), 0)
    c64 = lax.broadcasted_iota(jnp.int32, (LANES, GROUP_WIDTH), 1)
    ex64 = (c64 // SSM_HEAD_DIM == r64).astype(BF16)

    in_specs = (specs(fwd, 0) + specs(bwd, 1) + [row(0), row(1), row(0), row(1),
                pl.BlockSpec((None, 1, GROUP_WIDTH), lambda bi, g, s: (g, 0, 0)),
                full(tril), full(triu), full(ex64)])
    out_spec = lambda cidx: pl.BlockSpec((None, q, GROUP_WIDTH), lambda bi, g, s: (bi, cidx(s), g))
    return pl.pallas_call(
        _ssd_kernel,
        grid=(b, g_n, ns),
        in_specs=in_specs,
        out_specs=[out_spec(fwd), out_spec(bwd)],
        out_shape=[jax.ShapeDtypeStruct((b, t, SSM_WIDTH), F32)] * 2,
        scratch_shapes=[pltpu.VMEM((SSM_STATE, GROUP_WIDTH), F32)] * 2,
        compiler_params=_params(("parallel", "parallel", "arbitrary")),
        name="ssd_scan",
    )(u3, u3, u3, p3, u3, u3, u3, p3, bias4, bias4, alog4, alog4, dskip, tril, triu, ex64)


def _ssd_finish_kernel(yf_ref, yb_ref, z_ref, g_ref, o_ref):
    y = (yf_ref[...] + yb_ref[...]) * _silu(z_ref[...].astype(F32))
    outs = []
    for g in range(SSM_GROUPS):
        yg = y[:, g * GROUP_WIDTH:(g + 1) * GROUP_WIDTH]
        ms = jnp.mean(yg * yg, axis=-1, keepdims=True)
        outs.append((yg * lax.rsqrt(ms + EPS)) * g_ref[:, g * GROUP_WIDTH:(g + 1) * GROUP_WIDTH])
    o_ref[...] = jnp.concatenate(outs, axis=1).astype(BF16)


def _ssd_finish(yf, yb, p, norm_g):
    m = yf.shape[0]
    tm = _pick(m, (512, 256))
    return pl.pallas_call(
        _ssd_finish_kernel,
        grid=(m // tm,),
        in_specs=[
            pl.BlockSpec((tm, SSM_WIDTH), lambda i: (i, 0)),
            pl.BlockSpec((tm, SSM_WIDTH), lambda i: (i, 0)),
            pl.BlockSpec((tm, SSM_WIDTH), lambda i: (i, COL_Z // SSM_WIDTH)),
            pl.BlockSpec((1, SSM_WIDTH), lambda i: (0, 0)),
        ],
        out_specs=pl.BlockSpec((tm, SSM_WIDTH), lambda i: (i, 0)),
        out_shape=jax.ShapeDtypeStruct((m, SSM_WIDTH), BF16),
        compiler_params=_params(("parallel",)),
        name="ssd_finish",
    )(yf, yb, p, norm_g.reshape(1, SSM_WIDTH))


def _mm_resid_kernel(*refs, n_a, tm, geo):
    a_refs = refs[:n_a]
    w_refs = refs[n_a:2 * n_a]
    x_ref, gate_ref, o_ref = refs[2 * n_a:]
    i = pl.program_id(0)
    acc = None
    for a_ref, w_ref in zip(a_refs, w_refs):
        t = jnp.dot(a_ref[...], w_ref[...], preferred_element_type=F32)
        acc = t if acc is None else acc + t
    for u in range(tm // SUB):
        row = _mod_row(i * (tm // SUB) + u, geo)
        rs = slice(u * SUB, (u + 1) * SUB)
        o_ref[rs, :] = x_ref[rs, :] + gate_ref[pl.ds(row, 1), :] * acc[rs, :]


def _mm_resid(a_list, w_list, x, mod, gate_idx, geo, tn):
    m, d = x.shape
    tm = _pick(m, (1024, 512, 256))
    n_a = len(a_list)
    kern = functools.partial(_mm_resid_kernel, n_a=n_a, tm=tm, geo=geo)
    gate_blk = gate_idx * d // tn
    in_specs = ([pl.BlockSpec((tm, a.shape[1]), lambda i, j: (i, 0)) for a in a_list]
                + [pl.BlockSpec((w.shape[0], tn), lambda i, j: (0, j)) for w in w_list]
                + [pl.BlockSpec((tm, tn), lambda i, j: (i, j)),
                   pl.BlockSpec((mod.shape[0], tn), lambda i, j: (0, gate_blk + j))])
    return pl.pallas_call(
        kern,
        grid=(m // tm, d // tn),
        in_specs=in_specs,
        out_specs=pl.BlockSpec((tm, tn), lambda i, j: (i, j)),
        out_shape=jax.ShapeDtypeStruct((m, d), F32),
        compiler_params=_params(("parallel", "parallel")),
        name="matmul_gated_residual",
    )(*a_list, *w_list, x, mod)


def _ffn_in_kernel(x_ref, g_ref, mod_ref, w1_ref, w3_ref, o_ref, h_ref, *, tm, d, geo):
    i = pl.program_id(0)

    @pl.when(pl.program_id(1) == 0)
    def _():
        for u in range(tm // SUB):
            row = _mod_row(i * (tm // SUB) + u, geo)
            rs = slice(u * SUB, (u + 1) * SUB)
            shift = mod_ref[pl.ds(row, 1), 3 * d:4 * d]
            scale = mod_ref[pl.ds(row, 1), 4 * d:5 * d]
            h_ref[rs, :] = _normmod(x_ref[rs, :], g_ref[...], shift, scale).astype(BF16)

    h = h_ref[...]
    a = jnp.dot(h, w1_ref[...], preferred_element_type=F32)
    b = jnp.dot(h, w3_ref[...], preferred_element_type=F32)
    o_ref[...] = (_silu(a) * b).astype(BF16)


def _ffn_in(x, g, mod, w1, w3, geo):
    m, d = x.shape
    ff = w1.shape[1]
    tm = _pick(m, (1024, 512, 256))
    tf = _pick(ff, (512, 256, 128))
    kern = functools.partial(_ffn_in_kernel, tm=tm, d=d, geo=geo)
    return pl.pallas_call(
        kern,
        grid=(m // tm, ff // tf),
        in_specs=[
            pl.BlockSpec((tm, d), lambda i, j: (i, 0)),
            pl.BlockSpec((1, d), lambda i, j: (0, 0)),
            pl.BlockSpec(mod.shape, lambda i, j: (0, 0)),
            pl.BlockSpec((d, tf), lambda i, j: (0, j)),
            pl.BlockSpec((d, tf), lambda i, j: (0, j)),
        ],
        out_specs=pl.BlockSpec((tm, tf), lambda i, j: (i, j)),
        out_shape=jax.ShapeDtypeStruct((m, ff), BF16),
        scratch_shapes=[pltpu.VMEM((tm, d), BF16)],
        compiler_params=_params(("parallel", "arbitrary")),
        name="ffn_in",
    )(x, g.reshape(1, d), mod, w1, w3)


def _store_row_tiles(dst_ref, src_ref, rt):
    tm = src_ref.shape[0]
    for a in range(rt):
        dst_ref[pl.ds(a, tm, stride=rt), :] = src_ref[:, a * LANES:(a + 1) * LANES]


def _load_row_tiles(src_ref, row0, tm, rt):
    return jnp.concatenate([src_ref[pl.ds(row0 * rt + a, tm, stride=rt), :] for a in range(rt)], axis=1)


def _router_kernel(x_ref, g_ref, mod_ref, rw_ref, h_ref, r_ref, hs_ref, *, tm, d, n_exp, geo):
    i = pl.program_id(0)
    for u in range(tm // SUB):
        row = _mod_row(i * (tm // SUB) + u, geo)
        rs = slice(u * SUB, (u + 1) * SUB)
        shift = mod_ref[pl.ds(row, 1), 3 * d:4 * d]
        scale = mod_ref[pl.ds(row, 1), 4 * d:5 * d]
        hs_ref[rs, :] = _normmod(x_ref[rs, :], g_ref[...], shift, scale)
    _store_row_tiles(h_ref, hs_ref, d // LANES)
    logits = jnp.dot(hs_ref[...].astype(BF16), rw_ref[...], preferred_element_type=F32)
    lane = lax.broadcasted_iota(jnp.int32, logits.shape, 1)
    neg = jnp.float32(-1e30)
    logits = jnp.where(lane < n_exp, logits, neg)
    m1 = jnp.max(logits, axis=-1, keepdims=True)
    i1 = jnp.min(jnp.where(logits == m1, lane, LANES), axis=-1, keepdims=True)
    rest = jnp.where(lane == i1, neg, logits)
    m2 = jnp.max(rest, axis=-1, keepdims=True)
    i2 = jnp.min(jnp.where(rest == m2, lane, LANES), axis=-1, keepdims=True)
    t = jnp.exp(m2 - m1)
    g1 = 1.0 / (1.0 + t)
    g2 = t * g1
    out = jnp.where(lane == 0, i1.astype(F32),
                    jnp.where(lane == 1, i2.astype(F32),
                              jnp.where(lane == 2, g1, jnp.where(lane == 3, g2, 0.0))))
    r_ref[...] = out


def _router(x, g, mod, rw_pad, n_exp, geo):
    m, d = x.shape
    rt = d // LANES
    tm = _pick(m, (512, 256))
    kern = functools.partial(_router_kernel, tm=tm, d=d, n_exp=n_exp, geo=geo)
    return pl.pallas_call(
        kern,
        grid=(m // tm,),
        in_specs=[
            pl.BlockSpec((tm, d), lambda i: (i, 0)),
            pl.BlockSpec((1, d), lambda i: (0, 0)),
            pl.BlockSpec(mod.shape, lambda i: (0, 0)),
            pl.BlockSpec((d, LANES), lambda i: (0, 0)),
        ],
        out_specs=[pl.BlockSpec((tm * rt, LANES), lambda i: (i, 0)), pl.BlockSpec((tm, LANES), lambda i: (i, 0))],
        out_shape=[jax.ShapeDtypeStruct((m * rt, LANES), F32), jax.ShapeDtypeStruct((m, LANES), F32)],
        scratch_shapes=[pltpu.VMEM((tm, d), F32)],
        compiler_params=_params(("parallel",)),
        name="moe_router",
    )(x, g.reshape(1, d), mod, rw_pad)


GATHER_ROWS = 512
GATHER_UNROLL = 8


def _gather_kernel(*refs, n_out, rows, rt):
    idx_refs = refs[:n_out]
    src_ref = refs[n_out]
    dst_refs = refs[n_out + 1:2 * n_out + 1]
    sem = refs[2 * n_out + 1]

    def copy(k, r):
        src_row = pl.multiple_of(idx_refs[k][r] * rt, rt)
        dst_row = pl.multiple_of(r * rt, rt)
        return pltpu.make_async_copy(src_ref.at[pl.ds(src_row, rt), :], dst_refs[k].at[pl.ds(dst_row, rt), :], sem)

    def start(b, c):
        for u in range(GATHER_UNROLL):
            for k in range(n_out):
                copy(k, b * GATHER_UNROLL + u).start()
        return c

    def wait(b, c):
        for u in range(GATHER_UNROLL):
            for k in range(n_out):
                copy(k, b * GATHER_UNROLL + u).wait()
        return c

    lax.fori_loop(0, rows // GATHER_UNROLL, start, 0)
    lax.fori_loop(0, rows // GATHER_UNROLL, wait, 0)


def _gather_rows(src, idx_list, rt):
    n = idx_list[0].shape[0]
    rows = GATHER_ROWS
    assert n % rows == 0
    n_out = len(idx_list)
    kern = functools.partial(_gather_kernel, n_out=n_out, rows=rows, rt=rt)
    outs = pl.pallas_call(
        kern,
        grid=(n // rows,),
        in_specs=([pl.BlockSpec((rows,), lambda i: (i,), memory_space=pltpu.SMEM) for _ in idx_list]
                  + [pl.BlockSpec(memory_space=pl.ANY)]),
        out_specs=[pl.BlockSpec((rows * rt, LANES), lambda i: (i, 0)) for _ in idx_list],
        out_shape=[jax.ShapeDtypeStruct((n * rt, LANES), src.dtype) for _ in idx_list],
        scratch_shapes=[pltpu.SemaphoreType.DMA(())],
        compiler_params=_params(("arbitrary",), has_side_effects=True),
        name="gather_rows",
    )(*idx_list, src)
    return outs


def _moe_in_kernel(te_ref, nu_ref, h_ref, w1_ref, w3_ref, o_ref, hb_ref):
    t = pl.program_id(0)
    j = pl.program_id(1)
    tm, d = hb_ref.shape

    @pl.when(t < nu_ref[0])
    def _():
        @pl.when(j == 0)
        def _():
            hb_ref[...] = _load_row_tiles(h_ref, 0, tm, d // LANES).astype(BF16)

        h = hb_ref[...]
        a = jnp.dot(h, w1_ref[...], preferred_element_type=F32)
        b = jnp.dot(h, w3_ref[...], preferred_element_type=F32)
        o_ref[...] = (_silu(a) * b).astype(BF16)

    @pl.when(t >= nu_ref[0])
    def _():
        o_ref[...] = jnp.zeros_like(o_ref)


def _moe_in(hs, w1, w3, layer, tile_expert, n_used, tm):
    d = w1.shape[2]
    rt = d // LANES
    s = hs.shape[0] // rt
    ff = w1.shape[3]
    tf = _pick(ff, (1408, 512, 256, 128))
    n_ff = ff // tf

    def w_map(t, j, te, nu):
        return (layer, te[t], 0, jnp.where(t < nu[0], j, n_ff - 1))

    return pl.pallas_call(
        _moe_in_kernel,
        grid_spec=pltpu.PrefetchScalarGridSpec(
            num_scalar_prefetch=2,
            grid=(s // tm, n_ff),
            in_specs=[
                pl.BlockSpec((tm * rt, LANES), lambda t, j, te, nu: (t, 0)),
                pl.BlockSpec((None, None, d, tf), w_map),
                pl.BlockSpec((None, None, d, tf), w_map),
            ],
            out_specs=pl.BlockSpec((tm, tf), lambda t, j, te, nu: (t, j)),
            scratch_shapes=[pltpu.VMEM((tm, d), BF16)],
        ),
        out_shape=jax.ShapeDtypeStruct((s, ff), BF16),
        compiler_params=_params(("arbitrary", "arbitrary")),
        name="moe_experts_in",
    )(tile_expert, n_used, hs, w1, w3)


def _moe_out_kernel(te_ref, nu_ref, u_ref, w2_ref, o_ref, *, rt):
    t = pl.program_id(0)
    j = pl.program_id(1)
    tm = u_ref.shape[0]
    cols = w2_ref.shape[1] // LANES
    valid = t < nu_ref[0]

    @pl.when(valid)
    def _():
        y = jnp.dot(u_ref[...], w2_ref[...], preferred_element_type=F32)
        for a in range(cols):
            o_ref[pl.ds(j * cols + a, tm, stride=rt), :] = y[:, a * LANES:(a + 1) * LANES]

    @pl.when(jnp.logical_and(jnp.logical_not(valid), j == 0))
    def _():
        o_ref[...] = jnp.zeros_like(o_ref)


def _moe_out(u, w2, layer, tile_expert, n_used, tm):
    s, ff = u.shape
    d = w2.shape[3]
    rt = d // LANES
    tn = _pick(d, (1024, 512, 256, 128))
    n_n = d // tn

    def w_map(t, j, te, nu):
        return (layer, te[t], 0, jnp.where(t < nu[0], j, n_n - 1))

    return pl.pallas_call(
        functools.partial(_moe_out_kernel, rt=rt),
        grid_spec=pltpu.PrefetchScalarGridSpec(
            num_scalar_prefetch=2,
            grid=(s // tm, n_n),
            in_specs=[
                pl.BlockSpec((tm, ff), lambda t, j, te, nu: (t, 0)),
                pl.BlockSpec((None, None, ff, tn), w_map),
            ],
            out_specs=pl.BlockSpec((tm * rt, LANES), lambda t, j, te, nu: (t, 0)),
        ),
        out_shape=jax.ShapeDtypeStruct((s * rt, LANES), F32),
        compiler_params=_params(("arbitrary", "arbitrary")),
        name="moe_experts_out",
    )(tile_expert, n_used, u, w2)


def _moe_combine_kernel(x_ref, ya_ref, yb_ref, route_ref, gate_ref, o_ref, *, tm, geo):
    i = pl.program_id(0)
    rt = x_ref.shape[1] // LANES
    for u in range(tm // SUB):
        row = _mod_row(i * (tm // SUB) + u, geo)
        rs = slice(u * SUB, (u + 1) * SUB)
        g1 = route_ref[rs, TOP_K:TOP_K + 1]
        g2 = route_ref[rs, TOP_K + 1:TOP_K + 2]
        y = g1 * _load_row_tiles(ya_ref, u * SUB, SUB, rt) + g2 * _load_row_tiles(yb_ref, u * SUB, SUB, rt)
        o_ref[rs, :] = x_ref[rs, :] + gate_ref[pl.ds(row, 1), :] * y


def _moe_combine(x, ya, yb, route, mod, geo):
    m, d = x.shape
    rt = d // LANES
    tm = _pick(m, (512, 256))
    kern = functools.partial(_moe_combine_kernel, tm=tm, geo=geo)
    blk = pl.BlockSpec((tm, d), lambda i: (i, 0))
    blk3 = pl.BlockSpec((tm * rt, LANES), lambda i: (i, 0))
    return pl.pallas_call(
        kern,
        grid=(m // tm,),
        in_specs=[blk, blk3, blk3, pl.BlockSpec((tm, LANES), lambda i: (i, 0)),
                  pl.BlockSpec((mod.shape[0], d), lambda i: (0, 5))],
        out_specs=blk,
        out_shape=jax.ShapeDtypeStruct((m, d), F32),
        compiler_params=_params(("parallel",)),
        name="moe_combine",
    )(x, ya, yb, route, mod)


def _moe_plan(route, n_exp, tm, n_tiles):
    m = route.shape[0]
    e = route[:, 0:TOP_K].astype(jnp.int32).reshape(-1)
    onehot = (e[:, None] == jnp.arange(n_exp, dtype=jnp.int32)[None, :]).astype(jnp.int32)
    csum = jnp.cumsum(onehot, axis=0)
    rank = jnp.sum(csum * onehot, axis=1) - 1
    counts = csum[-1]
    tiles_per = (counts + tm - 1) // tm
    tile_end = jnp.cumsum(tiles_per)
    tile_start = tile_end - tiles_per
    pos = tile_start[e] * tm + rank
    n_used = tile_end[-1]
    n_slots = n_tiles * tm
    token = jnp.arange(TOP_K * m, dtype=jnp.int32) // TOP_K
    slot_token = jnp.zeros((n_slots,), jnp.int32).at[pos].set(token)
    tile_ids = jnp.arange(n_tiles, dtype=jnp.int32)
    tile_expert = jnp.sum((tile_ids[:, None] >= tile_end[None, :]).astype(jnp.int32), axis=1)
    last_expert = jnp.sum((n_used - 1 >= tile_end).astype(jnp.int32))
    tile_expert = jnp.where(tile_ids < n_used, tile_expert, last_expert).astype(jnp.int32)
    return pos.astype(jnp.int32), slot_token, tile_expert, n_used.reshape(1).astype(jnp.int32)


def _moe_layer(x, g, mod, rw_pad, w1, w3, w2, layer, geo):
    m, d = x.shape
    rt = d // LANES
    n_exp = w1.shape[1]
    tm = 512
    n_tiles = -(-(TOP_K * m) // tm) + n_exp
    n_tiles = -(-(n_tiles * tm) // GATHER_ROWS) * GATHER_ROWS // tm
    h, route = _router(x, g, mod, rw_pad, n_exp, geo)
    pos, slot_token, tile_expert, n_used = _moe_plan(route, n_exp, tm, n_tiles)
    (hs,) = _gather_rows(h, [slot_token], rt)
    u = _moe_in(hs, w1, w3, layer, tile_expert, n_used, tm)
    y = _moe_out(u, w2, layer, tile_expert, n_used, tm)
    mp = -(-m // GATHER_ROWS) * GATHER_ROWS
    pos2 = jnp.pad(pos.reshape(m, TOP_K), ((0, mp - m), (0, 0)))
    ya, yb = _gather_rows(y, [pos2[:, 0], pos2[:, 1]], rt)
    return _moe_combine(x, ya[:m * rt], yb[:m * rt], route, mod, geo)


def _final_norm_kernel(x_ref, g_ref, o_ref):
    x = x_ref[...]
    ms = jnp.mean(x * x, axis=-1, keepdims=True)
    o_ref[...] = (x * lax.rsqrt(ms + EPS)) * g_ref[...]


def _final_norm(x3, g, ctx_len):
    b, t, d = x3.shape
    seq = t - ctx_len
    off = ctx_len // SUB
    return pl.pallas_call(
        _final_norm_kernel,
        grid=(b, seq // SUB),
        in_specs=[pl.BlockSpec((None, SUB, d), lambda bi, i: (bi, off + i, 0)),
                  pl.BlockSpec((1, d), lambda bi, i: (0, 0))],
        out_specs=pl.BlockSpec((None, SUB, d), lambda bi, i: (bi, i, 0)),
        out_shape=jax.ShapeDtypeStruct((b, seq, d), F32),
        compiler_params=_params(("parallel", "parallel")),
        name="final_norm",
    )(x3, g.reshape(1, d))


def _rope_tables(batch, seq, ctx_len):
    rows = seq // GRID_W
    row_pos = jnp.repeat(jnp.arange(rows, dtype=F32), GRID_W)
    col_pos = jnp.tile(jnp.arange(GRID_W, dtype=F32), rows)
    inv_freq = ROPE_THETA ** (-jnp.arange(ROPE_PAIRS_PER_AXIS, dtype=F32) / ROPE_PAIRS_PER_AXIS)
    ang = jnp.concatenate([row_pos[:, None] * inv_freq, col_pos[:, None] * inv_freq], axis=-1)
    cos = jnp.cos(ang)
    sin = jnp.sin(ang)
    zero = jnp.zeros_like(sin)
    cos128 = jnp.tile(cos, (1, 4))
    sa128 = jnp.tile(jnp.concatenate([-sin, zero], axis=-1), (1, 2))
    sb128 = jnp.tile(jnp.concatenate([zero, sin], axis=-1), (1, 2))

    def stream(tab, ctx_val):
        ctx_rows = jnp.full((ctx_len, LANES), ctx_val, F32)
        return jnp.tile(jnp.concatenate([ctx_rows, tab], axis=0), (batch, 1))

    return stream(cos128, 1.0), stream(sa128, 0.0), stream(sb128, 0.0)


def _pad_w_in(w_in):
    depth, d, _ = w_in.shape
    main = w_in[:, :, :COL_DT]
    dt = w_in[:, :, COL_DT:COL_DT + 2 * SSM_HEADS].reshape(depth, d, 2 * SSM_GROUPS, SSM_HEADS_PER_GROUP)
    dt = jnp.pad(dt, ((0, 0), (0, 0), (0, 0), (0, LANES - SSM_HEADS_PER_GROUP)))
    return jnp.concatenate([main, dt.reshape(depth, d, 2 * SSM_GROUPS * LANES)], axis=-1).astype(BF16)


def _lane_rows(v):
    depth = v.shape[0]
    v = v.reshape(depth, 2 * SSM_GROUPS, 1, SSM_HEADS_PER_GROUP)
    return jnp.pad(v, ((0, 0), (0, 0), (0, 0), (0, LANES - SSM_HEADS_PER_GROUP)))


def kernel(x, c, ctx, c_ctx, w_mod, b_mod, norm_mix_g, w_in, lambda_q1, lambda_k1, lambda_q2, lambda_k2, subln_g, conv_w, conv_b, a_log, dt_bias, d_skip, ssm_norm_g, w_out, norm_ffn_g, ffn_w1, ffn_w3, ffn_w2, router_w, moe_w1, moe_w3, moe_w2, final_g):
    batch, seq, d = x.shape
    ctx_len = ctx.shape[1]
    depth = w_mod.shape[0]
    t = ctx_len + seq
    m = batch * t
    assert ctx_len % SUB == 0 and seq % SUB == 0 and seq % GRID_W == 0 and batch < 8
    geo = (t // SUB, ctx_len // SUB, batch)

    w_in_p = _pad_w_in(w_in)
    w_out_b = w_out.astype(BF16)
    moe_w1_b, moe_w3_b, moe_w2_b = moe_w1.astype(BF16), moe_w3.astype(BF16), moe_w2.astype(BF16)
    n_exp = router_w.shape[-1]
    rw_pad = jnp.pad(router_w, ((0, 0), (0, 0), (0, LANES - n_exp))).astype(BF16)

    c_all = jnp.zeros((8, d), F32).at[:batch].set(c).at[batch].set(c_ctx)
    mod_all = _mod_table(c_all, w_mod, b_mod)
    cos, sa, sb = _rope_tables(batch, seq, ctx_len)
    bias_rows = _lane_rows(dt_bias)
    alog_rows = _lane_rows(a_log)
    dskip_rows = jnp.repeat(d_skip, SSM_HEAD_DIM, axis=-1).reshape(depth, SSM_GROUPS, 1, GROUP_WIDTH)

    xs = jnp.concatenate([ctx, x], axis=1).reshape(m, d)
    for li in range(depth):
        lambda_init = 0.8 - 0.6 * math.exp(-0.3 * li)
        mod = mod_all[li]
        lam = (jnp.exp(jnp.sum(lambda_q1[li] * lambda_k1[li])) - jnp.exp(jnp.sum(lambda_q2[li] * lambda_k2[li]))
               + lambda_init).reshape(1).astype(F32)

        p = _in_proj(xs, norm_mix_g[li], mod, w_in_p[li], cos, sa, sb, geo)
        p3 = p.reshape(batch, t, IN_COLS_PAD)
        att = _attention(p3, lam, subln_g[li], lambda_init, ctx_len)
        u3 = _conv_silu(p3, conv_w[li], conv_b[li], ctx_len)
        yf, yb = _ssd_scan(u3, p3, bias_rows[li], alog_rows[li], dskip_rows[li], ctx_len)
        ssm = _ssd_finish(yf.reshape(m, SSM_WIDTH), yb.reshape(m, SSM_WIDTH), p, ssm_norm_g[li])
        xs = _mm_resid([att.reshape(m, ATT_WIDTH), ssm], [w_out_b[li, :ATT_WIDTH], w_out_b[li, ATT_WIDTH:]],
                       xs, mod, 2, geo, tn=1024)

        j = li // 2
        if li % 2 == 0:
            uu = _ffn_in(xs, norm_ffn_g[li], mod, ffn_w1[j].astype(BF16), ffn_w3[j].astype(BF16), geo)
            xs = _mm_resid([uu], [ffn_w2[j].astype(BF16)], xs, mod, 5, geo, tn=512)
        else:
            xs = _moe_layer(xs, norm_ffn_g[li], mod, rw_pad[j], moe_w1_b, moe_w3_b, moe_w2_b, j, geo)
    return _final_norm(xs.reshape(batch, t, d), final_g, ctx_len)
```

```python
import functools
import math

import jax
import jax.numpy as jnp
from jax import lax
from jax.experimental import pallas as pl
from jax.experimental.pallas import tpu as pltpu

F32 = jnp.float32
BF16 = jnp.bfloat16

GRID_W = 64
EPS = 1e-6
N_MOD = 6
ATT_HEADS = 8
ATT_QK_DIM = 64
ATT_V_DIM = 128
ATT_WIDTH = ATT_HEADS * ATT_V_DIM
ROPE_THETA = 10000.0
ROPE_PAIRS_PER_AXIS = ATT_QK_DIM // 4
SSM_HEAD_DIM = 64
SSM_GROUPS = 2
SSM_HEADS_PER_GROUP = 8
SSM_HEADS = SSM_GROUPS * SSM_HEADS_PER_GROUP
SSM_WIDTH = SSM_HEADS * SSM_HEAD_DIM
SSM_STATE = 128
SSM_CONV = 5
SSM_CHUNK = 128
GROUP_WIDTH = SSM_WIDTH // SSM_GROUPS
XBC_COLS = SSM_WIDTH + 2 * SSM_GROUPS * SSM_STATE
TOP_K = 2

LANES = 128
SUB = 256
ATT_HEADS_PER_STEP = 4
VMEM_LIMIT = 52 * 1024 * 1024

COL_Q = 0
COL_K = COL_Q + ATT_WIDTH
COL_V = COL_K + ATT_WIDTH
COL_Z = COL_V + ATT_WIDTH
COL_XBC = COL_Z + SSM_WIDTH
COL_DT = COL_XBC + XBC_COLS
IN_COLS_PAD = COL_DT + 2 * SSM_GROUPS * LANES


def _pick(n, cands):
    for c in cands:
        if n % c == 0:
            return c
    raise ValueError(f"no tile in {cands} divides {n}")


def _params(sem, vmem=VMEM_LIMIT, **kw):
    return pltpu.CompilerParams(dimension_semantics=sem, vmem_limit_bytes=vmem, **kw)


def _mod_row(sub_idx, geo):
    subs_per_batch, ctx_subs, n_batch = geo
    b = sub_idx // subs_per_batch
    w = sub_idx % subs_per_batch
    return jnp.where(w < ctx_subs, n_batch, b)


def _normmod(x, g, shift, scale):
    ms = jnp.mean(x * x, axis=-1, keepdims=True)
    hn = (x * lax.rsqrt(ms + EPS)) * g
    return hn * (1.0 + scale) + shift


def _silu(x):
    return x * (1.0 / (1.0 + jnp.exp(-x)))


def _mod_kernel(c_ref, w_ref, b_ref, o_ref):
    s = _silu(c_ref[...]).astype(BF16)
    o_ref[...] = jnp.dot(s, w_ref[...].astype(BF16), preferred_element_type=F32) + b_ref[...]


def _mod_table(c_all, w_mod, b_mod):
    depth, d, n = w_mod.shape
    tn = _pick(n, (1024, 512, 256, 128))
    return pl.pallas_call(
        _mod_kernel,
        grid=(depth, n // tn),
        in_specs=[
            pl.BlockSpec((8, d), lambda l, j: (0, 0)),
            pl.BlockSpec((None, d, tn), lambda l, j: (l, 0, j)),
            pl.BlockSpec((None, 1, tn), lambda l, j: (l, 0, j)),
        ],
        out_specs=pl.BlockSpec((None, 8, tn), lambda l, j: (l, 0, j)),
        out_shape=jax.ShapeDtypeStruct((depth, 8, n), F32),
        compiler_params=_params(("parallel", "parallel")),
        name="mod_table",
    )(c_all, w_mod, b_mod.reshape(depth, 1, n))


def _in_proj_kernel(x_ref, g_ref, mod_ref, w_ref, cos_ref, sa_ref, sb_ref, o_ref, h_ref, *, tm, tn, d, geo):
    i = pl.program_id(0)
    j = pl.program_id(1)

    @pl.when(j == 0)
    def _():
        for u in range(tm // SUB):
            row = _mod_row(i * (tm // SUB) + u, geo)
            rs = slice(u * SUB, (u + 1) * SUB)
            shift = mod_ref[pl.ds(row, 1), 0:d]
            scale = mod_ref[pl.ds(row, 1), d:2 * d]
            h_ref[rs, :] = _normmod(x_ref[rs, :], g_ref[...], shift, scale).astype(BF16)

    acc = jnp.dot(h_ref[...], w_ref[...], preferred_element_type=F32)

    def rope(scale):
        outs = []
        for hs in range(tn // LANES):
            a = acc[:, hs * LANES:(hs + 1) * LANES]
            r = (a * cos_ref[...] + pltpu.roll(a, LANES - 32, 1) * sa_ref[...]
                 + pltpu.roll(a, 32, 1) * sb_ref[...])
            outs.append(r * scale)
        return jnp.concatenate(outs, axis=1)

    @pl.when(j == COL_Q // tn)
    def _():
        o_ref[...] = rope(ATT_QK_DIM ** -0.5 * math.log2(math.e)).astype(BF16)

    @pl.when(j == COL_K // tn)
    def _():
        o_ref[...] = rope(1.0).astype(BF16)

    @pl.when(j >= COL_V // tn)
    def _():
        o_ref[...] = acc.astype(BF16)


def _in_proj(x, g, mod, w, cos, sa, sb, geo):
    m, d = x.shape
    n = w.shape[1]
    tm = _pick(m, (1024, 512, 256))
    tn = ATT_WIDTH
    kern = functools.partial(_in_proj_kernel, tm=tm, tn=tn, d=d, geo=geo)
    return pl.pallas_call(
        kern,
        grid=(m // tm, n // tn),
        in_specs=[
            pl.BlockSpec((tm, d), lambda i, j: (i, 0)),
            pl.BlockSpec((1, d), lambda i, j: (0, 0)),
            pl.BlockSpec(mod.shape, lambda i, j: (0, 0)),
            pl.BlockSpec((d, tn), lambda i, j: (0, j)),
            pl.BlockSpec((tm, LANES), lambda i, j: (i, 0)),
            pl.BlockSpec((tm, LANES), lambda i, j: (i, 0)),
            pl.BlockSpec((tm, LANES), lambda i, j: (i, 0)),
        ],
        out_specs=pl.BlockSpec((tm, tn), lambda i, j: (i, j)),
        out_shape=jax.ShapeDtypeStruct((m, n), BF16),
        scratch_shapes=[pltpu.VMEM((tm, d), BF16)],
        compiler_params=_params(("parallel", "arbitrary")),
        name="in_proj",
    )(x, g.reshape(1, d), mod, w, cos, sa, sb)


def _attn_kernel(lam_ref, q_ref, k_ref, v_ref, g_ref, o_ref, s_ref, vt_ref, *, tq, tk, ctx_tiles, ctx_chunks,
                 all_chunks, out_scale):
    qi = pl.program_id(2)
    hp = q_ref.shape[1] // LANES
    heads = [slice(h * LANES, (h + 1) * LANES) for h in range(hp)]

    @pl.when(qi == 0)
    def _():
        for h, hs in enumerate(heads):
            for c in range(all_chunks):
                vt_ref[h, c] = v_ref[c * tk:(c + 1) * tk, hs].astype(F32).T.astype(BF16)

    row = lax.broadcasted_iota(jnp.int32, (LANES, tq), 0)
    q2ts = []
    for hs in heads:
        qt = q_ref[:, hs].astype(F32).T
        q2ts.append(jnp.concatenate([jnp.where(row < ATT_QK_DIM, qt, 0.0), jnp.where(row >= ATT_QK_DIM, qt, 0.0)],
                                    axis=1).astype(BF16))

    def fold(x):
        return x.reshape(tk // 8, 8, x.shape[-1])

    def scores(h, n_chunks):
        mm = None
        for c in range(n_chunks):
            s = jnp.dot(k_ref[c * tk:(c + 1) * tk, heads[h]], q2ts[h], preferred_element_type=F32)
            s_ref[h, c] = s
            sf = jnp.max(fold(s), axis=0)
            mm = sf if mm is None else jnp.maximum(mm, sf)
        return jnp.max(mm, axis=0, keepdims=True)

    def values(h, n_chunks, m):
        ps = None
        acc = None
        for c in range(n_chunks):
            p = jnp.exp2(s_ref[h, c] - m)
            pf = jnp.sum(fold(p), axis=0)
            ps = pf if ps is None else ps + pf
            pv = jnp.dot(vt_ref[h, c], p.astype(BF16), preferred_element_type=F32)
            acc = pv if acc is None else acc + pv
        ot = acc * (1.0 / jnp.sum(ps, axis=0, keepdims=True))
        o = (ot[:, :tq] - lam_ref[0] * ot[:, tq:]).T
        ms = jnp.mean(o * o, axis=-1, keepdims=True)
        o = (o * lax.rsqrt(ms + EPS)) * g_ref[...] * out_scale
        o_ref[:, heads[h]] = o.astype(BF16)

    def attend(n_chunks):
        ms = [scores(h, n_chunks) for h in range(hp)]
        for h in range(hp):
            values(h, n_chunks, ms[h])

    @pl.when(qi < ctx_tiles)
    def _():
        attend(ctx_chunks)

    @pl.when(qi >= ctx_tiles)
    def _():
        attend(all_chunks)


def _attention(p3, lam, subln_g, lambda_init, ctx_len):
    b, t, _ = p3.shape
    tq = SUB // 2
    tk = SUB
    hp = ATT_HEADS_PER_STEP
    w = hp * LANES
    kern = functools.partial(_attn_kernel, tq=tq, tk=tk, ctx_tiles=ctx_len // tq, ctx_chunks=ctx_len // tk,
                             all_chunks=t // tk, out_scale=1.0 - lambda_init)
    return pl.pallas_call(
        kern,
        grid=(b, ATT_HEADS // hp, t // tq),
        in_specs=[
            pl.BlockSpec(memory_space=pltpu.SMEM),
            pl.BlockSpec((None, tq, w), lambda bi, h, qi: (bi, qi, COL_Q // w + h)),
            pl.BlockSpec((None, t, w), lambda bi, h, qi: (bi, 0, COL_K // w + h)),
            pl.BlockSpec((None, t, w), lambda bi, h, qi: (bi, 0, COL_V // w + h)),
            pl.BlockSpec((1, LANES), lambda bi, h, qi: (0, 0)),
        ],
        out_specs=pl.BlockSpec((None, tq, w), lambda bi, h, qi: (bi, qi, h)),
        out_shape=jax.ShapeDtypeStruct((b, t, ATT_WIDTH), BF16),
        scratch_shapes=[pltpu.VMEM((hp, t // tk, tk, 2 * tq), F32), pltpu.VMEM((hp, t // tk, LANES, tk), BF16)],
        compiler_params=_params(("parallel", "parallel", "arbitrary")),
        name="diff_attention",
    )(lam, p3, p3, p3, subln_g.reshape(1, LANES))


CONV_HALO = 16


def _conv_kernel(prev_ref, cur_ref, next_ref, w_ref, b_ref, o_ref, *, rb, ctx_blocks, n_blocks):
    i = pl.program_id(1)
    prev_ok = jnp.logical_and(i != 0, i != ctx_blocks)
    next_ok = jnp.logical_and(i != ctx_blocks - 1, i != n_blocks - 1)
    prev = jnp.where(prev_ok, prev_ref[...].astype(F32), 0.0)
    nxt = jnp.where(next_ok, next_ref[...].astype(F32), 0.0)
    xc = jnp.concatenate([prev, cur_ref[...].astype(F32), nxt], axis=0)
    pad = SSM_CONV // 2
    acc = b_ref[...] + w_ref[0:1, :] * xc[CONV_HALO - pad:CONV_HALO - pad + rb, :]
    for k in range(1, SSM_CONV):
        s = CONV_HALO - pad + k
        acc = acc + w_ref[k:k + 1, :] * xc[s:s + rb, :]
    o_ref[...] = _silu(acc).astype(BF16)


def _conv_silu(p3, conv_w, conv_b, ctx_len):
    b, t, _ = p3.shape
    rb = SUB
    tc = 512
    hb = rb // CONV_HALO
    n_blocks = t // rb
    c0 = COL_XBC // tc
    kern = functools.partial(_conv_kernel, rb=rb, ctx_blocks=ctx_len // rb, n_blocks=n_blocks)
    last_halo = t // CONV_HALO - 1
    return pl.pallas_call(
        kern,
        grid=(b, n_blocks, XBC_COLS // tc),
        in_specs=[
            pl.BlockSpec((None, CONV_HALO, tc), lambda bi, i, j: (bi, jnp.maximum(i * hb - 1, 0), c0 + j)),
            pl.BlockSpec((None, rb, tc), lambda bi, i, j: (bi, i, c0 + j)),
            pl.BlockSpec((None, CONV_HALO, tc), lambda bi, i, j: (bi, jnp.minimum((i + 1) * hb, last_halo), c0 + j)),
            pl.BlockSpec((SSM_CONV, tc), lambda bi, i, j: (0, j)),
            pl.BlockSpec((1, tc), lambda bi, i, j: (0, j)),
        ],
        out_specs=pl.BlockSpec((None, rb, tc), lambda bi, i, j: (bi, i, j)),
        out_shape=jax.ShapeDtypeStruct((b, t, XBC_COLS), BF16),
        compiler_params=_params(("parallel", "parallel", "parallel")),
        name="conv_silu",
    )(p3, p3, p3, conv_w, conv_b.reshape(1, XBC_COLS))


def _split_bf16(a, pieces):
    parts = []
    r = a
    for _ in range(pieces):
        p = r.astype(BF16)
        parts.append(p)
        r = r - p.astype(F32)
    return parts


def _dot_sel(a, sel, pieces):
    out = None
    for p in _split_bf16(a, pieces):
        t = jnp.dot(p, sel, preferred_element_type=F32)
        out = t if out is None else out + t
    return out


def _sel_dot(sel, a, pieces):
    out = None
    for p in _split_bf16(a, pieces):
        t = jnp.dot(sel, p, preferred_element_type=F32)
        out = t if out is None else out + t
    return out


def _ssd_chunk(xs, bm, cm, dtb, bias, alog, h_ref, tri, ex64, reverse):
    q = xs.shape[0]
    hp = SSM_HEADS_PER_GROUP
    dt = jax.nn.softplus(dtb.astype(F32) + bias)
    a = dt * (-jnp.exp(alog))
    acs = _sel_dot(tri, a, 3)
    acs_t = acs.T
    end = acs[0:1, :] if reverse else acs[q - 1:q, :]
    dt_e = _dot_sel(dt, ex64, 2)
    eacs_e = _dot_sel(jnp.exp(acs), ex64, 2)
    dte_e = _dot_sel(jnp.exp(end - acs), ex64, 2)
    xsf = xs.astype(F32)
    xdt = xsf * dt_e
    cb = lax.dot_general(cm, bm, (((1,), (1,)), ((), ())), preferred_element_type=F32)
    qi = lax.broadcasted_iota(jnp.int32, (q, q), 0)
    si = lax.broadcasted_iota(jnp.int32, (q, q), 1)
    mask = (si >= qi) if reverse else (si <= qi)
    lane = lax.broadcasted_iota(jnp.int32, (q, LANES), 1)
    ys = []
    for j in range(hp // 2):
        xp = xdt[:, j * LANES:(j + 1) * LANES]
        halves = (jnp.where(lane < SSM_HEAD_DIM, xp, 0.0).astype(BF16),
                  jnp.where(lane >= SSM_HEAD_DIM, xp, 0.0).astype(BF16))
        acc = None
        for k in range(2):
            r = 2 * j + k
            seg = jnp.broadcast_to(acs[:, r:r + 1], (q, LANES)) - acs_t[r:r + 1, :]
            dec = jnp.exp(jnp.where(mask, seg, -1e30))
            t = jnp.dot((cb * dec).astype(BF16), halves[k], preferred_element_type=F32)
            acc = t if acc is None else acc + t
        ys.append(acc)
    y_diag = jnp.concatenate(ys, axis=1)
    h = h_ref[...]
    y_off = jnp.dot(cm, h.astype(BF16), preferred_element_type=F32) * eacs_e
    cdec = eacs_e[0:1, :] if reverse else eacs_e[q - 1:q, :]
    xd = (xdt * dte_e).astype(BF16)
    h_ref[...] = h * cdec + lax.dot_general(bm, xd, (((0,), (0,)), ((), ())), preferred_element_type=F32)
    return y_diag + y_off, xsf


def _ssd_kernel(xsf_ref, bmf_ref, cmf_ref, dtf_ref, xsb_ref, bmb_ref, cmb_ref, dtb_ref,
                biasf_ref, biasb_ref, alogf_ref, alogb_ref, dskip_ref, tril_ref, triu_ref, ex64_ref,
                yf_ref, yb_ref, hf_ref, hb_ref):
    @pl.when(pl.program_id(2) == 0)
    def _():
        hf_ref[...] = jnp.zeros_like(hf_ref)
        hb_ref[...] = jnp.zeros_like(hb_ref)

    yf, xsf = _ssd_chunk(xsf_ref[...], bmf_ref[...], cmf_ref[...], dtf_ref[...], biasf_ref[...], alogf_ref[...],
                         hf_ref, tril_ref[...], ex64_ref[...], False)
    yf_ref[...] = yf + dskip_ref[...] * xsf
    yb, _ = _ssd_chunk(xsb_ref[...], bmb_ref[...], cmb_ref[...], dtb_ref[...], biasb_ref[...], alogb_ref[...],
                       hb_ref, triu_ref[...], ex64_ref[...], True)
    yb_ref[...] = yb


def _ssd_scan(u3, p3, bias4, alog4, dskip, ctx_len):
    b, t, _ = u3.shape
    q = SSM_CHUNK
    ns = t // q
    ncc = ctx_len // q
    g_n = SSM_GROUPS

    def fwd(s):
        return s

    def bwd(s):
        return jnp.where(s < ncc, ncc - 1 - s, ns - 1 + ncc - s)

    xs_blk = GROUP_WIDTH // LANES
    b_col = SSM_WIDTH // LANES
    c_col = b_col + g_n
    dt_col = COL_DT // LANES
    del xs_blk

    def specs(cidx, direction):
        return [
            pl.BlockSpec((None, q, GROUP_WIDTH), lambda bi, g, s: (bi, cidx(s), g)),
            pl.BlockSpec((None, q, LANES), lambda bi, g, s: (bi, cidx(s), b_col + g)),
            pl.BlockSpec((None, q, LANES), lambda bi, g, s: (bi, cidx(s), c_col + g)),
            pl.BlockSpec((None, q, LANES), lambda bi, g, s: (bi, cidx(s), dt_col + direction * g_n + g)),
        ]

    row = lambda direction: pl.BlockSpec((None, 1, LANES), lambda bi, g, s: (direction * g_n + g, 0, 0))
    full = lambda arr: pl.BlockSpec(arr.shape, lambda bi, g, s: (0,) * arr.ndim)

    ri = lax.broadcasted_iota(jnp.int32, (q, q), 0)
    ci = lax.broadcasted_iota(jnp.int32, (q, q), 1)
    tril = (ci <= ri).astype(BF16)
    triu = (ci >= ri).astype(BF16)
    r64 = lax.broadcasted_iota(jnp.int32, (LANES, GROUP_WIDTH), 0)
    c64 = lax.broadcasted_iota(jnp.int32, (LANES, GROUP_WIDTH), 1)
    ex64 = (c64 // SSM_HEAD_DIM == r64).astype(BF16)

    in_specs = (specs(fwd, 0) + specs(bwd, 1) + [row(0), row(1), row(0), row(1),
                pl.BlockSpec((None, 1, GROUP_WIDTH), lambda bi, g, s: (g, 0, 0)),
                full(tril), full(triu), full(ex64)])
    out_spec = lambda cidx: pl.BlockSpec((None, q, GROUP_WIDTH), lambda bi, g, s: (bi, cidx(s), g))
    return pl.pallas_call(
        _ssd_kernel,
        grid=(b, g_n, ns),
        in_specs=in_specs,
        out_specs=[out_spec(fwd), out_spec(bwd)],
        out_shape=[jax.ShapeDtypeStruct((b, t, SSM_WIDTH), F32)] * 2,
        scratch_shapes=[pltpu.VMEM((SSM_STATE, GROUP_WIDTH), F32)] * 2,
        compiler_params=_params(("parallel", "parallel", "arbitrary")),
        name="ssd_scan",
    )(u3, u3, u3, p3, u3, u3, u3, p3, bias4, bias4, alog4, alog4, dskip, tril, triu, ex64)


def _ssd_finish_kernel(yf_ref, yb_ref, z_ref, g_ref, o_ref):
    y = (yf_ref[...] + yb_ref[...]) * _silu(z_ref[...].astype(F32))
    outs = []
    for g in range(SSM_GROUPS):
        yg = y[:, g * GROUP_WIDTH:(g + 1) * GROUP_WIDTH]
        ms = jnp.mean(yg * yg, axis=-1, keepdims=True)
        outs.append((yg * lax.rsqrt(ms + EPS)) * g_ref[:, g * GROUP_WIDTH:(g + 1) * GROUP_WIDTH])
    o_ref[...] = jnp.concatenate(outs, axis=1).astype(BF16)


def _ssd_finish(yf, yb, p, norm_g):
    m = yf.shape[0]
    tm = _pick(m, (512, 256))
    return pl.pallas_call(
        _ssd_finish_kernel,
        grid=(m // tm,),
        in_specs=[
            pl.BlockSpec((tm, SSM_WIDTH), lambda i: (i, 0)),
            pl.BlockSpec((tm, SSM_WIDTH), lambda i: (i, 0)),
            pl.BlockSpec((tm, SSM_WIDTH), lambda i: (i, COL_Z // SSM_WIDTH)),
            pl.BlockSpec((1, SSM_WIDTH), lambda i: (0, 0)),
        ],
        out_specs=pl.BlockSpec((tm, SSM_WIDTH), lambda i: (i, 0)),
        out_shape=jax.ShapeDtypeStruct((m, SSM_WIDTH), BF16),
        compiler_params=_params(("parallel",)),
        name="ssd_finish",
    )(yf, yb, p, norm_g.reshape(1, SSM_WIDTH))


def _mm_resid_kernel(*refs, n_a, tm, geo):
    a_refs = refs[:n_a]
    w_refs = refs[n_a:2 * n_a]
    x_ref, gate_ref, o_ref = refs[2 * n_a:]
    i = pl.program_id(0)
    acc = None
    for a_ref, w_ref in zip(a_refs, w_refs):
        t = jnp.dot(a_ref[...], w_ref[...], preferred_element_type=F32)
        acc = t if acc is None else acc + t
    for u in range(tm // SUB):
        row = _mod_row(i * (tm // SUB) + u, geo)
        rs = slice(u * SUB, (u + 1) * SUB)
        o_ref[rs, :] = x_ref[rs, :] + gate_ref[pl.ds(row, 1), :] * acc[rs, :]


def _mm_resid(a_list, w_list, x, mod, gate_idx, geo, tn):
    m, d = x.shape
    tm = _pick(m, (1024, 512, 256))
    n_a = len(a_list)
    kern = functools.partial(_mm_resid_kernel, n_a=n_a, tm=tm, geo=geo)
    gate_blk = gate_idx * d // tn
    in_specs = ([pl.BlockSpec((tm, a.shape[1]), lambda i, j: (i, 0)) for a in a_list]
                + [pl.BlockSpec((w.shape[0], tn), lambda i, j: (0, j)) for w in w_list]
                + [pl.BlockSpec((tm, tn), lambda i, j: (i, j)),
                   pl.BlockSpec((mod.shape[0], tn), lambda i, j: (0, gate_blk + j))])
    return pl.pallas_call(
        kern,
        grid=(m // tm, d // tn),
        in_specs=in_specs,
        out_specs=pl.BlockSpec((tm, tn), lambda i, j: (i, j)),
        out_shape=jax.ShapeDtypeStruct((m, d), F32),
        compiler_params=_params(("parallel", "parallel")),
        name="matmul_gated_residual",
    )(*a_list, *w_list, x, mod)


def _ffn_in_kernel(x_ref, g_ref, mod_ref, w1_ref, w3_ref, o_ref, h_ref, *, tm, d, geo):
    i = pl.program_id(0)

    @pl.when(pl.program_id(1) == 0)
    def _():
        for u in range(tm // SUB):
            row = _mod_row(i * (tm // SUB) + u, geo)
            rs = slice(u * SUB, (u + 1) * SUB)
            shift = mod_ref[pl.ds(row, 1), 3 * d:4 * d]
            scale = mod_ref[pl.ds(row, 1), 4 * d:5 * d]
            h_ref[rs, :] = _normmod(x_ref[rs, :], g_ref[...], shift, scale).astype(BF16)

    h = h_ref[...]
    a = jnp.dot(h, w1_ref[...], preferred_element_type=F32)
    b = jnp.dot(h, w3_ref[...], preferred_element_type=F32)
    o_ref[...] = (_silu(a) * b).astype(BF16)


def _ffn_in(x, g, mod, w1, w3, geo):
    m, d = x.shape
    ff = w1.shape[1]
    tm = _pick(m, (1024, 512, 256))
    tf = _pick(ff, (512, 256, 128))
    kern = functools.partial(_ffn_in_kernel, tm=tm, d=d, geo=geo)
    return pl.pallas_call(
        kern,
        grid=(m // tm, ff // tf),
        in_specs=[
            pl.BlockSpec((tm, d), lambda i, j: (i, 0)),
            pl.BlockSpec((1, d), lambda i, j: (0, 0)),
            pl.BlockSpec(mod.shape, lambda i, j: (0, 0)),
            pl.BlockSpec((d, tf), lambda i, j: (0, j)),
            pl.BlockSpec((d, tf), lambda i, j: (0, j)),
        ],
        out_specs=pl.BlockSpec((tm, tf), lambda i, j: (i, j)),
        out_shape=jax.ShapeDtypeStruct((m, ff), BF16),
        scratch_shapes=[pltpu.VMEM((tm, d), BF16)],
        compiler_params=_params(("parallel", "arbitrary")),
        name="ffn_in",
    )(x, g.reshape(1, d), mod, w1, w3)


def _store_row_tiles(dst_ref, src_ref, rt):
    tm = src_ref.shape[0]
    for a in range(rt):
        dst_ref[pl.ds(a, tm, stride=rt), :] = src_ref[:, a * LANES:(a + 1) * LANES]


def _load_row_tiles(src_ref, row0, tm, rt):
    return jnp.concatenate([src_ref[pl.ds(row0 * rt + a, tm, stride=rt), :] for a in range(rt)], axis=1)


def _router_kernel(x_ref, g_ref, mod_ref, rw_ref, h_ref, r_ref, hs_ref, *, tm, d, n_exp, geo):
    i = pl.program_id(0)
    for u in range(tm // SUB):
        row = _mod_row(i * (tm // SUB) + u, geo)
        rs = slice(u * SUB, (u + 1) * SUB)
        shift = mod_ref[pl.ds(row, 1), 3 * d:4 * d]
        scale = mod_ref[pl.ds(row, 1), 4 * d:5 * d]
        hs_ref[rs, :] = _normmod(x_ref[rs, :], g_ref[...], shift, scale)
    _store_row_tiles(h_ref, hs_ref, d // LANES)
    logits = jnp.dot(hs_ref[...].astype(BF16), rw_ref[...], preferred_element_type=F32)
    lane = lax.broadcasted_iota(jnp.int32, logits.shape, 1)
    neg = jnp.float32(-1e30)
    logits = jnp.where(lane < n_exp, logits, neg)
    m1 = jnp.max(logits, axis=-1, keepdims=True)
    i1 = jnp.min(jnp.where(logits == m1, lane, LANES), axis=-1, keepdims=True)
    rest = jnp.where(lane == i1, neg, logits)
    m2 = jnp.max(rest, axis=-1, keepdims=True)
    i2 = jnp.min(jnp.where(rest == m2, lane, LANES), axis=-1, keepdims=True)
    t = jnp.exp(m2 - m1)
    g1 = 1.0 / (1.0 + t)
    g2 = t * g1
    out = jnp.where(lane == 0, i1.astype(F32),
                    jnp.where(lane == 1, i2.astype(F32),
                              jnp.where(lane == 2, g1, jnp.where(lane == 3, g2, 0.0))))
    r_ref[...] = out


def _router(x, g, mod, rw_pad, n_exp, geo):
    m, d = x.shape
    rt = d // LANES
    tm = _pick(m, (512, 256))
    kern = functools.partial(_router_kernel, tm=tm, d=d, n_exp=n_exp, geo=geo)
    return pl.pallas_call(
        kern,
        grid=(m // tm,),
        in_specs=[
            pl.BlockSpec((tm, d), lambda i: (i, 0)),
            pl.BlockSpec((1, d), lambda i: (0, 0)),
            pl.BlockSpec(mod.shape, lambda i: (0, 0)),
            pl.BlockSpec((d, LANES), lambda i: (0, 0)),
        ],
        out_specs=[pl.BlockSpec((tm * rt, LANES), lambda i: (i, 0)), pl.BlockSpec((tm, LANES), lambda i: (i, 0))],
        out_shape=[jax.ShapeDtypeStruct((m * rt, LANES), F32), jax.ShapeDtypeStruct((m, LANES), F32)],
        scratch_shapes=[pltpu.VMEM((tm, d), F32)],
        compiler_params=_params(("parallel",)),
        name="moe_router",
    )(x, g.reshape(1, d), mod, rw_pad)


GATHER_ROWS = 512
GATHER_UNROLL = 8


def _gather_kernel(*refs, n_out, rows, rt):
    idx_refs = refs[:n_out]
    src_ref = refs[n_out]
    dst_refs = refs[n_out + 1:2 * n_out + 1]
    sem = refs[2 * n_out + 1]

    def copy(k, r):
        src_row = pl.multiple_of(idx_refs[k][r] * rt, rt)
        dst_row = pl.multiple_of(r * rt, rt)
        return pltpu.make_async_copy(src_ref.at[pl.ds(src_row, rt), :], dst_refs[k].at[pl.ds(dst_row, rt), :], sem)

    def start(b, c):
        for u in range(GATHER_UNROLL):
            for k in range(n_out):
                copy(k, b * GATHER_UNROLL + u).start()
        return c

    def wait(b, c):
        for u in range(GATHER_UNROLL):
            for k in range(n_out):
                copy(k, b * GATHER_UNROLL + u).wait()
        return c

    lax.fori_loop(0, rows // GATHER_UNROLL, start, 0)
    lax.fori_loop(0, rows // GATHER_UNROLL, wait, 0)


def _gather_rows(src, idx_list, rt):
    n = idx_list[0].shape[0]
    rows = GATHER_ROWS
    assert n % rows == 0
    n_out = len(idx_list)
    kern = functools.partial(_gather_kernel, n_out=n_out, rows=rows, rt=rt)
    outs = pl.pallas_call(
        kern,
        grid=(n // rows,),
        in_specs=([pl.BlockSpec((rows,), lambda i: (i,), memory_space=pltpu.SMEM) for _ in idx_list]
                  + [pl.BlockSpec(memory_space=pl.ANY)]),
        out_specs=[pl.BlockSpec((rows * rt, LANES), lambda i: (i, 0)) for _ in idx_list],
        out_shape=[jax.ShapeDtypeStruct((n * rt, LANES), src.dtype) for _ in idx_list],
        scratch_shapes=[pltpu.SemaphoreType.DMA(())],
        compiler_params=_params(("arbitrary",), has_side_effects=True),
        name="gather_rows",
    )(*idx_list, src)
    return outs


def _moe_in_kernel(te_ref, nu_ref, h_ref, w1_ref, w3_ref, o_ref, hb_ref):
    t = pl.program_id(0)
    j = pl.program_id(1)
    tm, d = hb_ref.shape

    @pl.when(t < nu_ref[0])
    def _():
        @pl.when(j == 0)
        def _():
            hb_ref[...] = _load_row_tiles(h_ref, 0, tm, d // LANES).astype(BF16)

        h = hb_ref[...]
        a = jnp.dot(h, w1_ref[...], preferred_element_type=F32)
        b = jnp.dot(h, w3_ref[...], preferred_element_type=F32)
        o_ref[...] = (_silu(a) * b).astype(BF16)

    @pl.when(t >= nu_ref[0])
    def _():
        o_ref[...] = jnp.zeros_like(o_ref)


def _moe_in(hs, w1, w3, layer, tile_expert, n_used, tm):
    d = w1.shape[2]
    rt = d // LANES
    s = hs.shape[0] // rt
    ff = w1.shape[3]
    tf = _pick(ff, (1408, 512, 256, 128))
    n_ff = ff // tf

    def w_map(t, j, te, nu):
        return (layer, te[t], 0, jnp.where(t < nu[0], j, n_ff - 1))

    return pl.pallas_call(
        _moe_in_kernel,
        grid_spec=pltpu.PrefetchScalarGridSpec(
            num_scalar_prefetch=2,
            grid=(s // tm, n_ff),
            in_specs=[
                pl.BlockSpec((tm * rt, LANES), lambda t, j, te, nu: (t, 0)),
                pl.BlockSpec((None, None, d, tf), w_map),
                pl.BlockSpec((None, None, d, tf), w_map),
            ],
            out_specs=pl.BlockSpec((tm, tf), lambda t, j, te, nu: (t, j)),
            scratch_shapes=[pltpu.VMEM((tm, d), BF16)],
        ),
        out_shape=jax.ShapeDtypeStruct((s, ff), BF16),
        compiler_params=_params(("arbitrary", "arbitrary")),
        name="moe_experts_in",
    )(tile_expert, n_used, hs, w1, w3)


def _moe_out_kernel(te_ref, nu_ref, u_ref, w2_ref, o_ref, *, rt):
    t = pl.program_id(0)
    j = pl.program_id(1)
    tm = u_ref.shape[0]
    cols = w2_ref.shape[1] // LANES
    valid = t < nu_ref[0]

    @pl.when(valid)
    def _():
        y = jnp.dot(u_ref[...], w2_ref[...], preferred_element_type=F32)
        for a in range(cols):
            o_ref[pl.ds(j * cols + a, tm, stride=rt), :] = y[:, a * LANES:(a + 1) * LANES]

    @pl.when(jnp.logical_and(jnp.logical_not(valid), j == 0))
    def _():
        o_ref[...] = jnp.zeros_like(o_ref)


def _moe_out(u, w2, layer, tile_expert, n_used, tm):
    s, ff = u.shape
    d = w2.shape[3]
    rt = d // LANES
    tn = _pick(d, (1024, 512, 256, 128))
    n_n = d // tn

    def w_map(t, j, te, nu):
        return (layer, te[t], 0, jnp.where(t < nu[0], j, n_n - 1))

    return pl.pallas_call(
        functools.partial(_moe_out_kernel, rt=rt),
        grid_spec=pltpu.PrefetchScalarGridSpec(
            num_scalar_prefetch=2,
            grid=(s // tm, n_n),
            in_specs=[
                pl.BlockSpec((tm, ff), lambda t, j, te, nu: (t, 0)),
                pl.BlockSpec((None, None, ff, tn), w_map),
            ],
            out_specs=pl.BlockSpec((tm * rt, LANES), lambda t, j, te, nu: (t, 0)),
        ),
        out_shape=jax.ShapeDtypeStruct((s * rt, LANES), F32),
        compiler_params=_params(("arbitrary", "arbitrary")),
        name="moe_experts_out",
    )(tile_expert, n_used, u, w2)


def _moe_combine_kernel(x_ref, ya_ref, yb_ref, route_ref, gate_ref, o_ref, *, tm, geo):
    i = pl.program_id(0)
    rt = x_ref.shape[1] // LANES
    for u in range(tm // SUB):
        row = _mod_row(i * (tm // SUB) + u, geo)
        rs = slice(u * SUB, (u + 1) * SUB)
        g1 = route_ref[rs, TOP_K:TOP_K + 1]
        g2 = route_ref[rs, TOP_K + 1:TOP_K + 2]
        y = g1 * _load_row_tiles(ya_ref, u * SUB, SUB, rt) + g2 * _load_row_tiles(yb_ref, u * SUB, SUB, rt)
        o_ref[rs, :] = x_ref[rs, :] + gate_ref[pl.ds(row, 1), :] * y


def _moe_combine(x, ya, yb, route, mod, geo):
    m, d = x.shape
    rt = d // LANES
    tm = _pick(m, (512, 256))
    kern = functools.partial(_moe_combine_kernel, tm=tm, geo=geo)
    blk = pl.BlockSpec((tm, d), lambda i: (i, 0))
    blk3 = pl.BlockSpec((tm * rt, LANES), lambda i: (i, 0))
    return pl.pallas_call(
        kern,
        grid=(m // tm,),
        in_specs=[blk, blk3, blk3, pl.BlockSpec((tm, LANES), lambda i: (i, 0)),
                  pl.BlockSpec((mod.shape[0], d), lambda i: (0, 5))],
        out_specs=blk,
        out_shape=jax.ShapeDtypeStruct((m, d), F32),
        compiler_params=_params(("parallel",)),
        name="moe_combine",
    )(x, ya, yb, route, mod)


def _moe_plan(route, n_exp, tm, n_tiles):
    m = route.shape[0]
    e = route[:, 0:TOP_K].astype(jnp.int32).reshape(-1)
    onehot = (e[:, None] == jnp.arange(n_exp, dtype=jnp.int32)[None, :]).astype(jnp.int32)
    csum = jnp.cumsum(onehot, axis=0)
    rank = jnp.sum(csum * onehot, axis=1) - 1
    counts = csum[-1]
    tiles_per = (counts + tm - 1) // tm
    tile_end = jnp.cumsum(tiles_per)
    tile_start = tile_end - tiles_per
    pos = tile_start[e] * tm + rank
    n_used = tile_end[-1]
    n_slots = n_tiles * tm
    token = jnp.arange(TOP_K * m, dtype=jnp.int32) // TOP_K
    slot_token = jnp.zeros((n_slots,), jnp.int32).at[pos].set(token)
    tile_ids = jnp.arange(n_tiles, dtype=jnp.int32)
    tile_expert = jnp.sum((tile_ids[:, None] >= tile_end[None, :]).astype(jnp.int32), axis=1)
    last_expert = jnp.sum((n_used - 1 >= tile_end).astype(jnp.int32))
    tile_expert = jnp.where(tile_ids < n_used, tile_expert, last_expert).astype(jnp.int32)
    return pos.astype(jnp.int32), slot_token, tile_expert, n_used.reshape(1).astype(jnp.int32)


def _moe_layer(x, g, mod, rw_pad, w1, w3, w2, layer, geo):
    m, d = x.shape
    rt = d // LANES
    n_exp = w1.shape[1]
    tm = 512
    n_tiles = -(-(TOP_K * m) // tm) + n_exp
    n_tiles = -(-(n_tiles * tm) // GATHER_ROWS) * GATHER_ROWS // tm
    h, route = _router(x, g, mod, rw_pad, n_exp, geo)
    pos, slot_token, tile_expert, n_used = _moe_plan(route, n_exp, tm, n_tiles)
    (hs,) = _gather_rows(h, [slot_token], rt)
    u = _moe_in(hs, w1, w3, layer, tile_expert, n_used, tm)
    y = _moe_out(u, w2, layer, tile_expert, n_used, tm)
    mp = -(-m // GATHER_ROWS) * GATHER_ROWS
    pos2 = jnp.pad(pos.reshape(m, TOP_K), ((0, mp - m), (0, 0)))
    ya, yb = _gather_rows(y, [pos2[:, 0], pos2[:, 1]], rt)
    return _moe_combine(x, ya[:m * rt], yb[:m * rt], route, mod, geo)


def _final_norm_kernel(x_ref, g_ref, o_ref):
    x = x_ref[...]
    ms = jnp.mean(x * x, axis=-1, keepdims=True)
    o_ref[...] = (x * lax.rsqrt(ms + EPS)) * g_ref[...]


def _final_norm(x3, g, ctx_len):
    b, t, d = x3.shape
    seq = t - ctx_len
    off = ctx_len // SUB
    return pl.pallas_call(
        _final_norm_kernel,
        grid=(b, seq // SUB),
        in_specs=[pl.BlockSpec((None, SUB, d), lambda bi, i: (bi, off + i, 0)),
                  pl.BlockSpec((1, d), lambda bi, i: (0, 0))],
        out_specs=pl.BlockSpec((None, SUB, d), lambda bi, i: (bi, i, 0)),
        out_shape=jax.ShapeDtypeStruct((b, seq, d), F32),
        compiler_params=_params(("parallel", "parallel")),
        name="final_norm",
    )(x3, g.reshape(1, d))


def _rope_tables(batch, seq, ctx_len):
    rows = seq // GRID_W
    row_pos = jnp.repeat(jnp.arange(rows, dtype=F32), GRID_W)
    col_pos = jnp.tile(jnp.arange(GRID_W, dtype=F32), rows)
    inv_freq = ROPE_THETA ** (-jnp.arange(ROPE_PAIRS_PER_AXIS, dtype=F32) / ROPE_PAIRS_PER_AXIS)
    ang = jnp.concatenate([row_pos[:, None] * inv_freq, col_pos[:, None] * inv_freq], axis=-1)
    cos = jnp.cos(ang)
    sin = jnp.sin(ang)
    zero = jnp.zeros_like(sin)
    cos128 = jnp.tile(cos, (1, 4))
    sa128 = jnp.tile(jnp.concatenate([-sin, zero], axis=-1), (1, 2))
    sb128 = jnp.tile(jnp.concatenate([zero, sin], axis=-1), (1, 2))

    def stream(tab, ctx_val):
        ctx_rows = jnp.full((ctx_len, LANES), ctx_val, F32)
        return jnp.tile(jnp.concatenate([ctx_rows, tab], axis=0), (batch, 1))

    return stream(cos128, 1.0), stream(sa128, 0.0), stream(sb128, 0.0)


def _pad_w_in(w_in):
    depth, d, _ = w_in.shape
    main = w_in[:, :, :COL_DT]
    dt = w_in[:, :, COL_DT:COL_DT + 2 * SSM_HEADS].reshape(depth, d, 2 * SSM_GROUPS, SSM_HEADS_PER_GROUP)
    dt = jnp.pad(dt, ((0, 0), (0, 0), (0, 0), (0, LANES - SSM_HEADS_PER_GROUP)))
    return jnp.concatenate([main, dt.reshape(depth, d, 2 * SSM_GROUPS * LANES)], axis=-1).astype(BF16)


def _lane_rows(v):
    depth = v.shape[0]
    v = v.reshape(depth, 2 * SSM_GROUPS, 1, SSM_HEADS_PER_GROUP)
    return jnp.pad(v, ((0, 0), (0, 0), (0, 0), (0, LANES - SSM_HEADS_PER_GROUP)))


def kernel(x, c, ctx, c_ctx, w_mod, b_mod, norm_mix_g, w_in, lambda_q1, lambda_k1, lambda_q2, lambda_k2, subln_g, conv_w, conv_b, a_log, dt_bias, d_skip, ssm_norm_g, w_out, norm_ffn_g, ffn_w1, ffn_w3, ffn_w2, router_w, moe_w1, moe_w3, moe_w2, final_g):
    batch, seq, d = x.shape
    ctx_len = ctx.shape[1]
    depth = w_mod.shape[0]
    t = ctx_len + seq
    m = batch * t
    assert ctx_len % SUB == 0 and seq % SUB == 0 and seq % GRID_W == 0 and batch < 8
    geo = (t // SUB, ctx_len // SUB, batch)

    w_in_p = _pad_w_in(w_in)
    w_out_b = w_out.astype(BF16)
    moe_w1_b, moe_w3_b, moe_w2_b = moe_w1.astype(BF16), moe_w3.astype(BF16), moe_w2.astype(BF16)
    n_exp = router_w.shape[-1]
    rw_pad = jnp.pad(router_w, ((0, 0), (0, 0), (0, LANES - n_exp))).astype(BF16)

    c_all = jnp.zeros((8, d), F32).at[:batch].set(c).at[batch].set(c_ctx)
    mod_all = _mod_table(c_all, w_mod, b_mod)
    cos, sa, sb = _rope_tables(batch, seq, ctx_len)
    bias_rows = _lane_rows(dt_bias)
    alog_rows = _lane_rows(a_log)
    dskip_rows = jnp.repeat(d_skip, SSM_HEAD_DIM, axis=-1).reshape(depth, SSM_GROUPS, 1, GROUP_WIDTH)

    xs = jnp.concatenate([ctx, x], axis=1).reshape(m, d)
    for li in range(depth):
        lambda_init = 0.8 - 0.6 * math.exp(-0.3 * li)
        mod = mod_all[li]
        lam = (jnp.exp(jnp.sum(lambda_q1[li] * lambda_k1[li])) - jnp.exp(jnp.sum(lambda_q2[li] * lambda_k2[li]))
               + lambda_init).reshape(1).astype(F32)

        p = _in_proj(xs, norm_mix_g[li], mod, w_in_p[li], cos, sa, sb, geo)
        p3 = p.reshape(batch, t, IN_COLS_PAD)
        att = _attention(p3, lam, subln_g[li], lambda_init, ctx_len)
        u3 = _conv_silu(p3, conv_w[li], conv_b[li], ctx_len)
        yf, yb = _ssd_scan(u3, p3, bias_rows[li], alog_rows[li], dskip_rows[li], ctx_len)
        ssm = _ssd_finish(yf.reshape(m, SSM_WIDTH), yb.reshape(m, SSM_WIDTH), p, ssm_norm_g[li])
        xs = _mm_resid([att.reshape(m, ATT_WIDTH), ssm], [w_out_b[li, :ATT_WIDTH], w_out_b[li, ATT_WIDTH:]],
                       xs, mod, 2, geo, tn=1024)

        j = li // 2
        if li % 2 == 0:
            uu = _ffn_in(xs, norm_ffn_g[li], mod, ffn_w1[j].astype(BF16), ffn_w3[j].astype(BF16), geo)
            xs = _mm_resid([uu], [ffn_w2[j].astype(BF16)], xs, mod, 5, geo, tn=512)
        else:
            xs = _moe_layer(xs, norm_ffn_g[li], mod, rw_pad[j], moe_w1_b, moe_w3_b, moe_w2_b, j, geo)
    return _final_norm(xs.reshape(batch, t, d), final_g, ctx_len)
```
